```python
import jax, jax.numpy as jnp
from jax import lax
import numpy as np

D_MODEL = 2048
BATCH = 4
SEQ = 2048
DEPTH = 1
DEC_BATCH = 128
DEC_SEQ = 8
PAST_LEN = 16384
PAGE_SIZE = 128

GLA_HEADS = 4
GLA_DK = D_MODEL // 2 // GLA_HEADS
GLA_DV = D_MODEL // GLA_HEADS
GATE_RANK = 16
GATE_TAU = 16.0
GLA_CHUNK = 64
POOL_WIDTH = D_MODEL // 2
POOL_GROUPS = 4
POOL_GROUP_W = POOL_WIDTH // POOL_GROUPS
POOL_OUT_GROUP_W = D_MODEL // POOL_GROUPS
POOL_WINDOWS = (2, 4, 8, 16)
POOL_BUF = max(POOL_WINDOWS) - 1
D_FF = 4 * D_MODEL
EPS = 1e-6

Q_W = GLA_HEADS * GLA_DK
K_W = GLA_HEADS * GLA_DK
V_W = GLA_HEADS * GLA_DV
R_W = GLA_HEADS * GLA_DV
A_W = GATE_RANK
U_W = POOL_WIDTH
GA_W = D_MODEL
GB_W = D_MODEL
IN_W = Q_W + K_W + V_W + R_W + A_W + U_W + GA_W + GB_W
IN_SPLITS = tuple(int(s) for s in np.cumsum([Q_W, K_W, V_W, R_W, A_W, U_W, GA_W]))

kernel_name = "gla_pool_gated_hybrid_step"


def rmsnorm(x, g):
    xf = x.astype(jnp.float32)
    y = xf * lax.rsqrt(jnp.mean(xf * xf, axis=-1, keepdims=True) + EPS)
    return (y * g.astype(jnp.float32)).astype(x.dtype)


def gla_scan(q, k, v, log_a, s0):
    B, T = q.shape[0], q.shape[1]
    c = min(GLA_CHUNK, T)
    n = -(-T // c)
    pad = n * c - T

    def blocks(t):
        t = jnp.pad(t.astype(jnp.float32), ((0, 0), (0, pad), (0, 0), (0, 0)))
        return t.reshape(B, n, c, GLA_HEADS, t.shape[-1]).transpose(1, 0, 3, 2, 4)

    mask = jnp.tril(jnp.ones((c, c), dtype=bool))

    def step(S, inp):
        qc, kc, vc, ac = inp
        b = jnp.cumsum(ac, axis=2)
        diff = b[:, :, :, None, :] - b[:, :, None, :, :]
        decay = jnp.exp(jnp.where(mask[:, :, None], diff, -jnp.inf))
        scores = jnp.einsum('bhtd,bhsd,bhtsd->bhts', qc, kc, decay)
        o = (jnp.einsum('bhts,bhsv->bhtv', scores, vc)
             + jnp.einsum('bhtd,bhdv->bhtv', qc * jnp.exp(b), S))
        b_last = b[:, :, -1:, :]
        S_new = (jnp.exp(b_last[:, :, 0, :])[..., None] * S
                 + jnp.einsum('bhsd,bhsv->bhdv', kc * jnp.exp(b_last - b), vc))
        return S_new, o

    S, o = lax.scan(step, s0.astype(jnp.float32),
                    (blocks(q), blocks(k), blocks(v), blocks(log_a)))
    o = o.transpose(1, 0, 3, 2, 4).reshape(B, n * c, GLA_HEADS, GLA_DV)[:, :T]
    return o, S


def multiscale_pool(u, buf, start_pos):
    T = u.shape[1]
    ext = jnp.concatenate([buf.astype(jnp.float32), u.astype(jnp.float32)], axis=1)
    csum = jnp.pad(jnp.cumsum(ext, axis=1), ((0, 0), (1, 0), (0, 0)))
    pos = start_pos + jnp.arange(T) + 1
    outs = []
    for g, w in enumerate(POOL_WINDOWS):
        lo, hi = g * POOL_GROUP_W, (g + 1) * POOL_GROUP_W
        s = (csum[:, POOL_BUF + 1:POOL_BUF + 1 + T, lo:hi]
             - csum[:, POOL_BUF + 1 - w:POOL_BUF + 1 - w + T, lo:hi])
        cnt = jnp.minimum(pos, w).astype(jnp.float32)[None, :, None]
        outs.append(s / cnt)
    pooled = jnp.concatenate(outs, axis=-1)
    return (pooled - u.astype(jnp.float32)).astype(u.dtype)


def layer(x, s_gla0, pool_buf, start_pos, norm_mix_g, w_in, w_alpha_up, b_alpha,
          gla_norm_g, pool_w, pool_scale, w_out, norm_mlp_g, w_up, w_down, norm_final_g):
    B, T, _ = x.shape
    h = rmsnorm(x, norm_mix_g)
    z = h @ w_in
    q, k, v, r, a_lr, u, ga, gb = jnp.split(z, IN_SPLITS, axis=-1)
    q = q.reshape(B, T, GLA_HEADS, GLA_DK) * (GLA_DK ** -0.5)
    k = k.reshape(B, T, GLA_HEADS, GLA_DK)
    v = v.reshape(B, T, GLA_HEADS, GLA_DV)
    a_logit = (a_lr @ w_alpha_up + b_alpha).astype(jnp.float32)
    log_a = (jax.nn.log_sigmoid(a_logit) / GATE_TAU).reshape(B, T, GLA_HEADS, GLA_DK)
    o, s_gla = gla_scan(q, k, v, log_a, s0=s_gla0)
    o = rmsnorm(o, gla_norm_g).reshape(B, T, V_W).astype(x.dtype)
    o_a = o * jax.nn.silu(r)
    p = multiscale_pool(u, pool_buf, start_pos)
    o_b = jnp.einsum('btgc,gcd->btgd', p.reshape(B, T, POOL_GROUPS, POOL_GROUP_W),
                     pool_w).reshape(B, T, D_MODEL) * pool_scale
    m = jax.nn.sigmoid(ga) * o_a + jax.nn.sigmoid(gb) * o_b
    x = x + m @ w_out
    h2 = rmsnorm(x, norm_mlp_g)
    x = x + jnp.square(jax.nn.relu(h2 @ w_up)) @ w_down
    y = rmsnorm(x, norm_final_g)
    new_buf = jnp.concatenate([pool_buf.astype(u.dtype), u], axis=1)[:, -POOL_BUF:]
    return y, s_gla, new_buf


def setup_inputs(seed: int = 0) -> dict:
    key = jax.random.key(seed)
    ks = jax.random.split(key, 16)
    f32 = jnp.float32
    nrm = lambda k, shape, scale: jax.random.normal(k, shape, f32) * scale
    return {
        "x_prompt": nrm(ks[0], (BATCH, SEQ, D_MODEL), 1.0),
        "x_sample": nrm(ks[1], (DEC_BATCH, DEC_SEQ, D_MODEL), 1.0),
        "state_gla": nrm(ks[2], (DEC_BATCH, GLA_HEADS, GLA_DK, GLA_DV), 1.0),
        "state_pool": nrm(ks[3], (DEC_BATCH, POOL_BUF, POOL_WIDTH), 1.0),
        "norm_mix_g": 1.0 + nrm(ks[4], (D_MODEL,), 0.02),
        "w_in": nrm(ks[5], (D_MODEL, IN_W), D_MODEL ** -0.5),
        "w_alpha_up": nrm(ks[6], (GATE_RANK, Q_W), GATE_RANK ** -0.5),
        "b_alpha": nrm(ks[7], (Q_W,), 0.1),
        "gla_norm_g": 1.0 + nrm(ks[8], (GLA_DV,), 0.02),
        "pool_w": nrm(ks[9], (POOL_GROUPS, POOL_GROUP_W, POOL_OUT_GROUP_W), POOL_GROUP_W ** -0.5),
        "pool_scale": 1.0 + nrm(ks[10], (D_MODEL,), 0.1),
        "w_out": nrm(ks[11], (D_MODEL, D_MODEL), D_MODEL ** -0.5),
        "norm_mlp_g": 1.0 + nrm(ks[12], (D_MODEL,), 0.02),
        "w_up": nrm(ks[13], (D_MODEL, D_FF), D_MODEL ** -0.5),
        "w_down": nrm(ks[14], (D_FF, D_MODEL), D_FF ** -0.5),
        "norm_final_g": 1.0 + nrm(ks[15], (D_MODEL,), 0.02),
    }


def reference(x_prompt, x_sample, state_gla, state_pool, norm_mix_g, w_in, w_alpha_up,
              b_alpha, gla_norm_g, pool_w, pool_scale, w_out, norm_mlp_g, w_up, w_down,
              norm_final_g):
    params = (norm_mix_g, w_in, w_alpha_up, b_alpha, gla_norm_g, pool_w, pool_scale,
              w_out, norm_mlp_g, w_up, w_down, norm_final_g)
    y_p, y_s = x_prompt, x_sample
    s_gla_p = jnp.zeros((BATCH, GLA_HEADS, GLA_DK, GLA_DV), jnp.float32)
    buf_p = jnp.zeros((BATCH, POOL_BUF, POOL_WIDTH), x_prompt.dtype)
    s_gla_s, buf_s = state_gla, state_pool
    for _ in range(DEPTH):
        y_p, s_gla_p, buf_p = layer(y_p, s_gla_p, buf_p, 0, *params)
        y_s, s_gla_s, buf_s = layer(y_s, s_gla_s, buf_s, PAST_LEN, *params)
    return (y_p, y_s, s_gla_p, buf_p, s_gla_s, buf_s)
```

```python
import functools

import jax
import jax.numpy as jnp
import numpy as np
from jax import lax
from jax.experimental import pallas as pl
from jax.experimental.pallas import tpu as pltpu

D_MODEL = 2048
GLA_HEADS = 4
GLA_DK = 256
GLA_DV = 512
GATE_RANK = 16
GATE_TAU = 16.0
POOL_WIDTH = 1024
POOL_GROUPS = 4
POOL_GROUP_W = 256
POOL_OUT_GROUP_W = 512
POOL_WINDOWS = (2, 4, 8, 16)
POOL_BUF = 15
D_FF = 8192
EPS = 1e-6
PAST_LEN = 16384

Q_W = GLA_HEADS * GLA_DK
V_W = GLA_HEADS * GLA_DV
QKVR_W = 2 * Q_W + 2 * V_W
A_OFF = QKVR_W
U_OFF = A_OFF + GATE_RANK
GA_OFF = U_OFF + POOL_WIDTH
GB_OFF = GA_OFF + D_MODEL
Z_Q, Z_K, Z_V, Z_R = 0, Q_W, 2 * Q_W, 2 * Q_W + V_W
Z_GA = QKVR_W
Z_GB = Z_GA + D_MODEL
Z_U = Z_GB + D_MODEL
Z_W = Z_U + POOL_WIDTH
assert Z_GA % D_MODEL == 0 and Z_GB % D_MODEL == 0 and Z_U % POOL_WIDTH == 0

HALO = 16
VMEM_LIMIT = 56 * 1024 * 1024

_F32 = jnp.float32
_BF16 = jnp.bfloat16


def _dot(a, b):
    return jnp.dot(a, b, preferred_element_type=_F32)


def _dot_nt(a, b):
    return lax.dot_general(a, b, (((1,), (1,)), ((), ())), preferred_element_type=_F32)


def _dot_tn(a, b):
    return lax.dot_general(a, b, (((0,), (0,)), ((), ())), preferred_element_type=_F32)


def _split3(x):
    hi = x.astype(_BF16)
    r1 = x - hi.astype(_F32)
    mid = r1.astype(_BF16)
    lo = (r1 - mid.astype(_F32)).astype(_BF16)
    return hi, mid, lo


def _rms_scale(x):
    return lax.rsqrt(jnp.mean(x * x, axis=-1, keepdims=True) + EPS)


def _in_proj_kernel(x_ref, g_ref, w_ref, wa_ref, wup_ref, ba_ref, z_ref, la_ref, h_scr):
    j = pl.program_id(1)

    @pl.when(j == 0)
    def _():
        x = x_ref[...]
        h = x * _rms_scale(x) * g_ref[...]
        hb = h.astype(_BF16)
        h_scr[...] = hb
        a_lr = _dot(hb, wa_ref[...])
        a_hi, a_mid, a_lo = _split3(a_lr)
        a3 = jnp.concatenate([a_hi, a_mid, a_lo], axis=1)
        a_logit = _dot(a3, wup_ref[...]) + ba_ref[...]
        ls = jnp.minimum(a_logit, 0.0) - jnp.log1p(jnp.exp(-jnp.abs(a_logit)))
        la_ref[...] = ls * (1.0 / GATE_TAU)

    z_ref[...] = _dot(h_scr[...], w_ref[...])


def _in_proj(x, g, w_packed, wa_pad, wup3, b_alpha, tm, tn):
    m = x.shape[0]
    grid = (m // tm, Z_W // tn)
    return pl.pallas_call(
        _in_proj_kernel,
        grid=grid,
        in_specs=[
            pl.BlockSpec((tm, D_MODEL), lambda i, j: (i, 0)),
            pl.BlockSpec((1, D_MODEL), lambda i, j: (0, 0)),
            pl.BlockSpec((D_MODEL, tn), lambda i, j: (0, j)),
            pl.BlockSpec((D_MODEL, 128), lambda i, j: (0, 0)),
            pl.BlockSpec((384, Q_W), lambda i, j: (0, 0)),
            pl.BlockSpec((1, Q_W), lambda i, j: (0, 0)),
        ],
        out_specs=[
            pl.BlockSpec((tm, tn), lambda i, j: (i, j)),
            pl.BlockSpec((tm, Q_W), lambda i, j: (i, 0)),
        ],
        out_shape=[
            jax.ShapeDtypeStruct((m, Z_W), _F32),
            jax.ShapeDtypeStruct((m, Q_W), _F32),
        ],
        scratch_shapes=[pltpu.VMEM((tm, D_MODEL), _BF16)],
        compiler_params=pltpu.CompilerParams(
            dimension_semantics=("arbitrary", "arbitrary"),
            vmem_limit_bytes=VMEM_LIMIT),
        name="in_proj",
    )(x, g, w_packed, wa_pad, wup3, b_alpha)


def _gla_consts(c):
    levels = int(np.log2(c))
    assert 2 ** levels == c
    t = np.arange(c)
    mats = [np.tril(np.ones((c, c), np.float32))]
    masks = [np.eye(c, dtype=np.float32)]
    for l in range(levels):
        h = 2 ** l
        blk = t // (2 * h)
        mid = blk * 2 * h + h - 1
        upper = (t // h) % 2 == 1
        m = np.zeros((c, c), np.float32)
        for row in range(c):
            if upper[row]:
                m[row, mid[row] + 1: row + 1] = 1.0
            else:
                m[row, row + 1: mid[row] + 1] = 1.0
        mats.append(m)
        masks.append(((blk[:, None] == blk[None, :]) & upper[:, None] & (~upper)[None, :])
                     .astype(np.float32))
    m_all = np.concatenate(mats, axis=0)
    m_cat = np.concatenate([m_all, m_all, m_all], axis=1)
    return jnp.asarray(m_cat, _BF16), jnp.asarray(np.stack(masks), _F32), levels


def _gla_chunk(q, k, v, la, s_state, m_cat, masks, ones_col, c, levels):
    hi, mid, lo = _split3(la)
    la3 = jnp.concatenate([hi, mid, lo], axis=0)
    eb = _dot(m_cat, la3)
    b = eb[0:c]
    kb = k.astype(_BF16)
    scores = jnp.where(masks[0] > 0, _dot_nt(q.astype(_BF16), kb), 0.0)
    for l in range(levels):
        x = jnp.exp(eb[(l + 1) * c:(l + 2) * c])
        s_l = _dot_nt((q * x).astype(_BF16), (k * x).astype(_BF16))
        scores = jnp.where(masks[l + 1] > 0, s_l, scores)
    vb = v.astype(_BF16)
    o = _dot(scores.astype(_BF16), vb) + _dot((q * jnp.exp(b)).astype(_BF16), s_state.astype(_BF16))
    b_last = b[c - 1:c, :]
    kd = (k * jnp.exp(b_last - b)).astype(_BF16)
    dcol = jnp.exp(_dot_tn(la3, ones_col))
    decay = jnp.concatenate([dcol] * (GLA_DV // 128), axis=1)
    s_new = decay * s_state + _dot_tn(kd, vb)
    return o, s_new


def _gla_post(o, r, g):
    o_n = o * _rms_scale(o) * g
    return o_n * (r * jax.nn.sigmoid(r))


def _gla_prompt_kernel(q_ref, k_ref, v_ref, r_ref, la_ref, g_ref, mc_ref, mk_ref, oc_ref,
                       oa_ref, sout_ref, s_scr, *, c, levels, n_sub):
    n = pl.program_id(2)

    @pl.when(n == 0)
    def _():
        s_scr[...] = jnp.zeros_like(s_scr)

    m_cat = mc_ref[...]
    masks = [mk_ref[i] for i in range(levels + 1)]
    ones_col = oc_ref[...]
    g = g_ref[...]

    def body(i, carry):
        rows = pl.ds(pl.multiple_of(i * c, c), c)
        q = q_ref[rows, :] * (GLA_DK ** -0.5)
        o, s_new = _gla_chunk(q, k_ref[rows, :], v_ref[rows, :], la_ref[rows, :], s_scr[...],
                              m_cat, masks, ones_col, c, levels)
        oa_ref[rows, :] = _gla_post(o, r_ref[rows, :], g)
        s_scr[...] = s_new
        return carry

    lax.fori_loop(0, n_sub, body, 0)

    @pl.when(n == pl.num_programs(2) - 1)
    def _():
        sout_ref[0, 0] = s_scr[...]


def _gla_prompt(z, la, gla_g, batch, seq, c, rows_per_step):
    m_cat, masks, levels = _gla_consts(c)
    ones_col = jnp.ones((3 * c, 128), _BF16)
    n_sub = rows_per_step // c
    steps = seq // rows_per_step
    kern = functools.partial(_gla_prompt_kernel, c=c, levels=levels, n_sub=n_sub)
    rb = lambda b, h, n: b * steps + n
    return pl.pallas_call(
        kern,
        grid=(batch, GLA_HEADS, steps),
        in_specs=[
            pl.BlockSpec((rows_per_step, GLA_DK), lambda b, h, n: (rb(b, h, n), Z_Q // GLA_DK + h)),
            pl.BlockSpec((rows_per_step, GLA_DK), lambda b, h, n: (rb(b, h, n), Z_K // GLA_DK + h)),
            pl.BlockSpec((rows_per_step, GLA_DV), lambda b, h, n: (rb(b, h, n), Z_V // GLA_DV + h)),
            pl.BlockSpec((rows_per_step, GLA_DV), lambda b, h, n: (rb(b, h, n), Z_R // GLA_DV + h)),
            pl.BlockSpec((rows_per_step, GLA_DK), lambda b, h, n: (rb(b, h, n), h)),
            pl.BlockSpec((1, GLA_DV), lambda b, h, n: (0, 0)),
            pl.BlockSpec(m_cat.shape, lambda b, h, n: (0, 0)),
            pl.BlockSpec(masks.shape, lambda b, h, n: (0, 0, 0)),
            pl.BlockSpec(ones_col.shape, lambda b, h, n: (0, 0)),
        ],
        out_specs=[
            pl.BlockSpec((rows_per_step, GLA_DV), lambda b, h, n: (rb(b, h, n), h)),
            pl.BlockSpec((1, 1, GLA_DK, GLA_DV), lambda b, h, n: (b, h, 0, 0)),
        ],
        out_shape=[
            jax.ShapeDtypeStruct((batch * seq, V_W), _F32),
            jax.ShapeDtypeStruct((batch, GLA_HEADS, GLA_DK, GLA_DV), _F32),
        ],
        scratch_shapes=[pltpu.VMEM((GLA_DK, GLA_DV), _F32)],
        compiler_params=pltpu.CompilerParams(
            dimension_semantics=("arbitrary", "arbitrary", "arbitrary"),
            vmem_limit_bytes=VMEM_LIMIT),
        name="gla_prompt",
    )(z, z, z, z, la, gla_g, m_cat, masks, ones_col)


def _gla_sample_kernel(q_ref, k_ref, v_ref, r_ref, la_ref, s_ref, g_ref, mc_ref, mk_ref, oc_ref,
                       oa_ref, sout_ref, *, c, seq, levels, bb):
    m_cat = mc_ref[...]
    masks = [mk_ref[i] for i in range(levels + 1)]
    ones_col = oc_ref[...]
    g = g_ref[...]

    pad = c - seq
    padded = lambda a: jnp.concatenate([a, jnp.zeros((pad, a.shape[1]), _F32)], axis=0)

    def body(i, carry):
        rows = pl.ds(pl.multiple_of(i * seq, seq), seq)
        for h in range(GLA_HEADS):
            dk = slice(h * GLA_DK, (h + 1) * GLA_DK)
            dv = slice(h * GLA_DV, (h + 1) * GLA_DV)
            q = padded(q_ref[rows, dk] * (GLA_DK ** -0.5))
            o, s_new = _gla_chunk(q, padded(k_ref[rows, dk]), padded(v_ref[rows, dv]),
                                  padded(la_ref[rows, dk]), s_ref[i, h], m_cat, masks, ones_col,
                                  c, levels)
            oa_ref[rows, dv] = _gla_post(o[0:seq], r_ref[rows, dv], g)
            sout_ref[i, h] = s_new
        return carry

    lax.fori_loop(0, bb, body, 0)


def _gla_sample(z, la, state, gla_g, row0, batch, seq, bb):
    c = 16
    assert seq <= c and seq % 8 == 0
    m_cat, masks, levels = _gla_consts(c)
    ones_col = jnp.ones((3 * c, 128), _BF16)
    rows = bb * seq
    rb0 = row0 // rows
    kern = functools.partial(_gla_sample_kernel, c=c, seq=seq, levels=levels, bb=bb)
    return pl.pallas_call(
        kern,
        grid=(batch // bb,),
        in_specs=[
            pl.BlockSpec((rows, Q_W), lambda i: (rb0 + i, Z_Q // Q_W)),
            pl.BlockSpec((rows, Q_W), lambda i: (rb0 + i, Z_K // Q_W)),
            pl.BlockSpec((rows, V_W), lambda i: (rb0 + i, Z_V // V_W)),
            pl.BlockSpec((rows, V_W), lambda i: (rb0 + i, Z_R // V_W)),
            pl.BlockSpec((rows, Q_W), lambda i: (rb0 + i, 0)),
            pl.BlockSpec((bb, GLA_HEADS, GLA_DK, GLA_DV), lambda i: (i, 0, 0, 0)),
            pl.BlockSpec((1, GLA_DV), lambda i: (0, 0)),
            pl.BlockSpec(m_cat.shape, lambda i: (0, 0)),
            pl.BlockSpec(masks.shape, lambda i: (0, 0, 0)),
            pl.BlockSpec(ones_col.shape, lambda i: (0, 0)),
        ],
        out_specs=[
            pl.BlockSpec((rows, V_W), lambda i: (i, 0)),
            pl.BlockSpec((bb, GLA_HEADS, GLA_DK, GLA_DV), lambda i: (i, 0, 0, 0)),
        ],
        out_shape=[
            jax.ShapeDtypeStruct((batch * seq, V_W), _F32),
            jax.ShapeDtypeStruct((batch, GLA_HEADS, GLA_DK, GLA_DV), _F32),
        ],
        compiler_params=pltpu.CompilerParams(
            dimension_semantics=("arbitrary",),
            vmem_limit_bytes=VMEM_LIMIT),
        name="gla_sample",
    )(z, z, z, z, la, state, gla_g, m_cat, masks, ones_col)


def _window_sums(ext_ref, t0, tt, lead):
    outs = []
    for gi, w in enumerate(POOL_WINDOWS):
        lanes = slice(gi * POOL_GROUP_W, (gi + 1) * POOL_GROUP_W)
        acc = ext_ref[lead + (pl.ds(t0, tt), lanes)]
        for j in range(1, w):
            acc = acc + ext_ref[lead + (pl.ds(t0 - j, tt), lanes)]
        outs.append(acc)
    return outs


def _pool_prompt_kernel(u_ref, p_ref, ext_scr, *, tt):
    n = pl.program_id(1)

    @pl.when(n == 0)
    def _():
        ext_scr[0:HALO, :] = jnp.zeros((HALO, POOL_WIDTH), _F32)

    u = u_ref[...]
    ext_scr[HALO:HALO + tt, :] = u
    sums = _window_sums(ext_scr, HALO, tt, ())
    pos = (n * tt + 1 + lax.broadcasted_iota(jnp.int32, (tt, 1), 0)).astype(_F32)
    for gi, w in enumerate(POOL_WINDOWS):
        lanes = slice(gi * POOL_GROUP_W, (gi + 1) * POOL_GROUP_W)
        cnt = jnp.minimum(pos, float(w))
        p_ref[:, lanes] = (sums[gi] / cnt - u[:, lanes]).astype(p_ref.dtype)
    ext_scr[0:HALO, :] = u[tt - HALO:tt, :]


def _pool_prompt(z, batch, seq, tt):
    steps = seq // tt
    return pl.pallas_call(
        functools.partial(_pool_prompt_kernel, tt=tt),
        grid=(batch, steps),
        in_specs=[pl.BlockSpec((tt, POOL_WIDTH), lambda b, n: (b * steps + n, Z_U // POOL_WIDTH))],
        out_specs=pl.BlockSpec((tt, POOL_WIDTH), lambda b, n: (b * steps + n, 0)),
        out_shape=jax.ShapeDtypeStruct((batch * seq, POOL_WIDTH), _BF16),
        scratch_shapes=[pltpu.VMEM((HALO + tt, POOL_WIDTH), _F32)],
        compiler_params=pltpu.CompilerParams(
            dimension_semantics=("arbitrary", "arbitrary"),
            vmem_limit_bytes=VMEM_LIMIT),
        name="pool_prompt",
    )(z)


def _pool_sample_kernel(u_ref, buf_ref, p_ref, nb_ref, ext_scr, *, bb, seq):
    u = u_ref[...]
    ext_scr[:, 0:HALO - POOL_BUF, :] = jnp.zeros((bb, HALO - POOL_BUF, POOL_WIDTH), _F32)
    ext_scr[:, HALO - POOL_BUF:HALO, :] = buf_ref[...]
    ext_scr[:, HALO:HALO + seq, :] = u
    sums = _window_sums(ext_scr, HALO, seq, (slice(None),))
    for gi, w in enumerate(POOL_WINDOWS):
        lanes = slice(gi * POOL_GROUP_W, (gi + 1) * POOL_GROUP_W)
        p_ref[:, :, lanes] = (sums[gi] / float(w) - u[:, :, lanes]).astype(p_ref.dtype)
    nb_ref[...] = ext_scr[:, HALO + seq - POOL_BUF:HALO + seq, :]


def _pool_sample(u3, buf, bb):
    batch, seq, _ = u3.shape
    assert PAST_LEN >= max(POOL_WINDOWS)
    return pl.pallas_call(
        functools.partial(_pool_sample_kernel, bb=bb, seq=seq),
        grid=(batch // bb,),
        in_specs=[
            pl.BlockSpec((bb, seq, POOL_WIDTH), lambda i: (i, 0, 0)),
            pl.BlockSpec((bb, POOL_BUF, POOL_WIDTH), lambda i: (i, 0, 0)),
        ],
        out_specs=[
            pl.BlockSpec((bb, seq, POOL_WIDTH), lambda i: (i, 0, 0)),
            pl.BlockSpec((bb, POOL_BUF, POOL_WIDTH), lambda i: (i, 0, 0)),
        ],
        out_shape=[
            jax.ShapeDtypeStruct((batch, seq, POOL_WIDTH), _BF16),
            jax.ShapeDtypeStruct((batch, POOL_BUF, POOL_WIDTH), _F32),
        ],
        scratch_shapes=[pltpu.VMEM((bb, HALO + seq, POOL_WIDTH), _F32)],
        compiler_params=pltpu.CompilerParams(
            dimension_semantics=("arbitrary",),
            vmem_limit_bytes=VMEM_LIMIT),
        name="pool_sample",
    )(u3, buf)


def _merge_out_kernel(x_ref, oa_ref, p_ref, ga_ref, gb_ref, pw_ref, ps_ref, wo_ref, g2_ref,
                      x1_ref, h2_ref):
    p = p_ref[...]
    ob = jnp.concatenate(
        [_dot(p[:, gi * POOL_GROUP_W:(gi + 1) * POOL_GROUP_W], pw_ref[gi])
         for gi in range(POOL_GROUPS)], axis=1) * ps_ref[...]
    m = jax.nn.sigmoid(ga_ref[...]) * oa_ref[...] + jax.nn.sigmoid(gb_ref[...]) * ob
    x1 = x_ref[...] + _dot(m.astype(_BF16), wo_ref[...])
    x1_ref[...] = x1
    h2_ref[...] = (x1 * _rms_scale(x1) * g2_ref[...]).astype(_BF16)


def _merge_out(x, oa, p, z, pool_w, pool_scale, w_out, g2, tm):
    m = x.shape[0]
    return pl.pallas_call(
        _merge_out_kernel,
        grid=(m // tm,),
        in_specs=[
            pl.BlockSpec((tm, D_MODEL), lambda i: (i, 0)),
            pl.BlockSpec((tm, D_MODEL), lambda i: (i, 0)),
            pl.BlockSpec((tm, POOL_WIDTH), lambda i: (i, 0)),
            pl.BlockSpec((tm, D_MODEL), lambda i: (i, Z_GA // D_MODEL)),
            pl.BlockSpec((tm, D_MODEL), lambda i: (i, Z_GB // D_MODEL)),
            pl.BlockSpec((POOL_GROUPS, POOL_GROUP_W, POOL_OUT_GROUP_W), lambda i: (0, 0, 0)),
            pl.BlockSpec((1, D_MODEL), lambda i: (0, 0)),
            pl.BlockSpec((D_MODEL, D_MODEL), lambda i: (0, 0)),
            pl.BlockSpec((1, D_MODEL), lambda i: (0, 0)),
        ],
        out_specs=[
            pl.BlockSpec((tm, D_MODEL), lambda i: (i, 0)),
            pl.BlockSpec((tm, D_MODEL), lambda i: (i, 0)),
        ],
        out_shape=[
            jax.ShapeDtypeStruct((m, D_MODEL), _F32),
            jax.ShapeDtypeStruct((m, D_MODEL), _BF16),
        ],
        compiler_params=pltpu.CompilerParams(
            dimension_semantics=("arbitrary",),
            vmem_limit_bytes=VMEM_LIMIT),
        name="merge_out",
    )(x, oa, p, z, z, pool_w, pool_scale, w_out, g2)


def _mlp_kernel(h2_ref, x1_ref, wu_ref, wd_ref, g_ref, y_ref, acc_scr):
    f = pl.program_id(1)

    @pl.when(f == 0)
    def _():
        acc_scr[...] = x1_ref[...]

    a = jnp.maximum(_dot(h2_ref[...], wu_ref[...]), 0.0)
    acc_scr[...] += _dot((a * a).astype(_BF16), wd_ref[...])

    @pl.when(f == pl.num_programs(1) - 1)
    def _():
        x2 = acc_scr[...]
        y_ref[...] = x2 * _rms_scale(x2) * g_ref[...]


def _mlp(h2, x1, w_up, w_down, g, tm, tf):
    m = h2.shape[0]
    return pl.pallas_call(
        _mlp_kernel,
        grid=(m // tm, D_FF // tf),
        in_specs=[
            pl.BlockSpec((tm, D_MODEL), lambda i, f: (i, 0)),
            pl.BlockSpec((tm, D_MODEL), lambda i, f: (i, 0)),
            pl.BlockSpec((D_MODEL, tf), lambda i, f: (0, f)),
            pl.BlockSpec((tf, D_MODEL), lambda i, f: (f, 0)),
            pl.BlockSpec((1, D_MODEL), lambda i, f: (0, 0)),
        ],
        out_specs=pl.BlockSpec((tm, D_MODEL), lambda i, f: (i, 0)),
        out_shape=jax.ShapeDtypeStruct((m, D_MODEL), _F32),
        scratch_shapes=[pltpu.VMEM((tm, D_MODEL), _F32)],
        compiler_params=pltpu.CompilerParams(
            dimension_semantics=("arbitrary", "arbitrary"),
            vmem_limit_bytes=VMEM_LIMIT),
        name="mlp",
    )(h2, x1, w_up, w_down, g)


def kernel(x_prompt, x_sample, state_gla, state_pool, norm_mix_g, w_in, w_alpha_up, b_alpha,
           gla_norm_g, pool_w, pool_scale, w_out, norm_mlp_g, w_up, w_down, norm_final_g):
    batch, seq, _ = x_prompt.shape
    dec_batch, dec_seq, _ = x_sample.shape
    n_p = batch * seq
    n_s = dec_batch * dec_seq

    x = jnp.concatenate([x_prompt.reshape(n_p, D_MODEL), x_sample.reshape(n_s, D_MODEL)], axis=0)
    w_packed = jnp.concatenate(
        [w_in[:, :QKVR_W], w_in[:, GA_OFF:], w_in[:, U_OFF:GA_OFF]], axis=1).astype(_BF16)
    wa_pad = jnp.pad(w_in[:, A_OFF:U_OFF], ((0, 0), (0, 128 - GATE_RANK))).astype(_BF16)
    wup_pad = jnp.pad(w_alpha_up, ((0, 128 - GATE_RANK), (0, 0)))
    wup3 = jnp.concatenate([wup_pad.astype(_BF16)] * 3, axis=0)
    row = lambda v: v.reshape(1, -1).astype(_F32)

    z, la = _in_proj(x, row(norm_mix_g), w_packed, wa_pad, wup3, row(b_alpha), tm=512, tn=512)

    oa_p, s_gla_p = _gla_prompt(z, la, row(gla_norm_g), batch, seq, c=64, rows_per_step=256)
    oa_s, s_gla_s = _gla_sample(z, la, state_gla, row(gla_norm_g), n_p, dec_batch, dec_seq, bb=2)
    oa = jnp.concatenate([oa_p, oa_s], axis=0)

    p_p = _pool_prompt(z, batch, seq, tt=256)
    u_s = z[n_p:, Z_U:Z_U + POOL_WIDTH].reshape(dec_batch, dec_seq, POOL_WIDTH)
    p_s, buf_s = _pool_sample(u_s, state_pool, bb=16)
    p = jnp.concatenate([p_p, p_s.reshape(n_s, POOL_WIDTH)], axis=0)
    buf_p = z[:n_p, Z_U:Z_U + POOL_WIDTH].reshape(batch, seq, POOL_WIDTH)[:, seq - POOL_BUF:, :]

    x1, h2 = _merge_out(x, oa, p, z, pool_w.astype(_BF16), row(pool_scale), w_out.astype(_BF16),
                        row(norm_mlp_g), tm=256)
    y = _mlp(h2, x1, w_up.astype(_BF16), w_down.astype(_BF16), row(norm_final_g), tm=512, tf=512)

    y_p = y[:n_p].reshape(batch, seq, D_MODEL)
    y_s = y[n_p:].reshape(dec_batch, dec_seq, D_MODEL)
    return (y_p, y_s, s_gla_p, buf_p, s_gla_s, buf_s)
```

```python
import functools

import jax
import jax.numpy as jnp
import numpy as np
from jax import lax
from jax.experimental import pallas as pl
from jax.experimental.pallas import tpu as pltpu

D_MODEL = 2048
GLA_HEADS = 4
GLA_DK = 256
GLA_DV = 512
GATE_RANK = 16
GATE_TAU = 16.0
POOL_WIDTH = 1024
POOL_GROUPS = 4
POOL_GROUP_W = 256
POOL_OUT_GROUP_W = 512
POOL_WINDOWS = (2, 4, 8, 16)
POOL_BUF = 15
D_FF = 8192
EPS = 1e-6
PAST_LEN = 16384

Q_W = GLA_HEADS * GLA_DK
V_W = GLA_HEADS * GLA_DV
QKVR_W = 2 * Q_W + 2 * V_W
A_OFF = QKVR_W
U_OFF = A_OFF + GATE_RANK
GA_OFF = U_OFF + POOL_WIDTH
Z_Q, Z_K, Z_V, Z_R = 0, Q_W, 2 * Q_W, 2 * Q_W + V_W

LANES = 128
PROJ_TN = 1024
HALO = 16
VMEM_LIMIT = 56 * 1024 * 1024

_F32 = jnp.float32
_BF16 = jnp.bfloat16


def _dot(a, b):
    return jnp.dot(a, b, preferred_element_type=_F32)


def _dot_nt(a, b):
    return lax.dot_general(a, b, (((1,), (1,)), ((), ())), preferred_element_type=_F32)


def _dot_tn(a, b):
    return lax.dot_general(a, b, (((0,), (0,)), ((), ())), preferred_element_type=_F32)


def _split3(x):
    hi = x.astype(_BF16)
    r1 = x - hi.astype(_F32)
    mid = r1.astype(_BF16)
    lo = (r1 - mid.astype(_F32)).astype(_BF16)
    return hi, mid, lo


def _rms_scale(x):
    return lax.rsqrt(jnp.mean(x * x, axis=-1, keepdims=True) + EPS)


def _params(*sem):
    return pltpu.CompilerParams(dimension_semantics=sem, vmem_limit_bytes=VMEM_LIMIT)


def _norm_gate_kernel(xp_ref, xs_ref, g_ref, wa_ref, wup_ref, ba_ref, h_ref, la_ref, *, prompt_tiles):
    i = pl.program_id(0)

    def compute(x_ref):
        x = x_ref[...]
        hb = (x * _rms_scale(x) * g_ref[...]).astype(_BF16)
        h_ref[...] = hb
        a_lr = _dot(hb, wa_ref[...])
        a_hi, a_mid, a_lo = _split3(a_lr)
        a3 = jnp.concatenate([a_hi, a_mid, a_lo], axis=1)
        a_logit = _dot(a3, wup_ref[...]) + ba_ref[...]
        ls = jnp.minimum(a_logit, 0.0) - jnp.log1p(jnp.exp(-jnp.abs(a_logit)))
        la_ref[...] = ls * (1.0 / GATE_TAU)

    @pl.when(i < prompt_tiles)
    def _():
        compute(xp_ref)

    @pl.when(i >= prompt_tiles)
    def _():
        compute(xs_ref)


def _norm_gate(xp, xs, g, wa_pad, wup3, b_alpha, tm):
    n_p, n_s = xp.shape[0], xs.shape[0]
    pt, st = n_p // tm, n_s // tm
    m = n_p + n_s
    const = lambda i: (0, 0)
    return pl.pallas_call(
        functools.partial(_norm_gate_kernel, prompt_tiles=pt),
        grid=(pt + st,),
        in_specs=[
            pl.BlockSpec((tm, D_MODEL), lambda i: (jnp.minimum(i, pt - 1), 0)),
            pl.BlockSpec((tm, D_MODEL), lambda i: (jnp.maximum(i - pt, 0), 0)),
            pl.BlockSpec((1, D_MODEL), const),
            pl.BlockSpec((D_MODEL, LANES), const),
            pl.BlockSpec((3 * LANES, Q_W), const),
            pl.BlockSpec((1, Q_W), const),
        ],
        out_specs=[
            pl.BlockSpec((tm, D_MODEL), lambda i: (i, 0)),
            pl.BlockSpec((tm, Q_W), lambda i: (i, 0)),
        ],
        out_shape=[
            jax.ShapeDtypeStruct((m, D_MODEL), _BF16),
            jax.ShapeDtypeStruct((m, Q_W), _F32),
        ],
        compiler_params=_params("arbitrary"),
        name="norm_gate",
    )(xp, xs, g, wa_pad, wup3, b_alpha)


def _proj_kernel(*refs, shift, tn):
    if shift:
        h_ref, wm_ref, wt_ref, z_ref, w_scr = refs
    else:
        h_ref, wm_ref, z_ref, w_scr = refs
    rc = 256

    @pl.when(pl.program_id(1) == 0)
    def _():
        for r in range(0, D_MODEL, rc):
            w = wm_ref[r:r + rc, :]
            if shift:
                wide = jnp.concatenate([w, wt_ref[r:r + rc, :]], axis=1)
                w = pltpu.roll(wide, tn + LANES - shift, axis=1)[:, :tn]
            w_scr[r:r + rc, :] = w.astype(_BF16)

    z_ref[...] = _dot(h_ref[...], w_scr[...]).astype(z_ref.dtype)


def _proj(h, w_in, col0, n_tiles, out_dtype, tm, tn=PROJ_TN):
    m = h.shape[0]
    shift = col0 % LANES
    blk0 = (col0 - shift) // tn
    assert (col0 - shift) % tn == 0
    in_specs = [
        pl.BlockSpec((tm, D_MODEL), lambda j, i: (i, 0)),
        pl.BlockSpec((D_MODEL, tn), lambda j, i: (0, blk0 + j)),
    ]
    args = [h, w_in]
    if shift:
        in_specs.append(pl.BlockSpec((D_MODEL, LANES), lambda j, i: (0, (blk0 + j + 1) * (tn // LANES))))
        args.append(w_in)
        assert col0 + n_tiles * tn <= w_in.shape[1]
    return pl.pallas_call(
        functools.partial(_proj_kernel, shift=shift, tn=tn),
        grid=(n_tiles, m // tm),
        in_specs=in_specs,
        out_specs=pl.BlockSpec((tm, tn), lambda j, i: (i, j)),
        out_shape=jax.ShapeDtypeStruct((m, n_tiles * tn), out_dtype),
        scratch_shapes=[pltpu.VMEM((D_MODEL, tn), _BF16)],
        compiler_params=_params("arbitrary", "arbitrary"),
        name=f"proj_c{col0}",
    )(*args)


def _gla_consts(c):
    levels = int(np.log2(c))
    assert 2 ** levels == c
    t = np.arange(c)
    mats = [np.tril(np.ones((c, c), np.float32))]
    masks = [np.eye(c, dtype=np.float32)]
    for l in range(levels):
        h = 2 ** l
        blk = t // (2 * h)
        mid = blk * 2 * h + h - 1
        upper = (t // h) % 2 == 1
        m = np.zeros((c, c), np.float32)
        for row in range(c):
            if upper[row]:
                m[row, mid[row] + 1: row + 1] = 1.0
            else:
                m[row, row + 1: mid[row] + 1] = 1.0
        mats.append(m)
        masks.append(((blk[:, None] == blk[None, :]) & upper[:, None] & (~upper)[None, :])
                     .astype(np.float32))
    m_all = np.concatenate(mats, axis=0)
    m_cat = np.concatenate([m_all, m_all, m_all], axis=1)
    return jnp.asarray(m_cat, _BF16), jnp.asarray(np.stack(masks), _F32), levels


def _gla_chunk(q, k, vb, la, s_state, m_cat, masks, ones_col, c, levels):
    hi, mid, lo = _split3(la)
    la3 = jnp.concatenate([hi, mid, lo], axis=0)
    eb = _dot(m_cat, la3)
    b = eb[0:c]
    scores = jnp.where(masks[0] > 0, _dot_nt(q.astype(_BF16), k.astype(_BF16)), 0.0)
    for l in range(levels):
        x = jnp.exp(eb[(l + 1) * c:(l + 2) * c])
        s_l = _dot_nt((q * x).astype(_BF16), (k * x).astype(_BF16))
        scores = jnp.where(masks[l + 1] > 0, s_l, scores)
    o = _dot(scores.astype(_BF16), vb) + _dot((q * jnp.exp(b)).astype(_BF16), s_state.astype(_BF16))
    b_last = b[c - 1:c, :]
    kd = (k * jnp.exp(b_last - b)).astype(_BF16)
    dcol = jnp.exp(_dot_tn(la3, ones_col))
    decay = jnp.concatenate([dcol] * (GLA_DV // LANES), axis=1)
    s_new = decay * s_state + _dot_tn(kd, vb)
    return o, s_new


def _gla_post(o, r, g):
    o_n = o * _rms_scale(o) * g
    return o_n * (r * jax.nn.sigmoid(r))


def _gla_prompt_kernel(q_ref, k_ref, v_ref, r_ref, la_ref, g_ref, mc_ref, mk_ref, oc_ref,
                       oa_ref, sout_ref, s_scr, *, c, levels, n_sub):
    n = pl.program_id(2)

    @pl.when(n == 0)
    def _():
        s_scr[...] = jnp.zeros_like(s_scr)

    m_cat = mc_ref[...]
    masks = [mk_ref[i] for i in range(levels + 1)]
    ones_col = oc_ref[...]
    g = g_ref[...]

    def body(i, carry):
        rows = pl.ds(pl.multiple_of(i * c, c), c)
        q = q_ref[rows, :].astype(_F32) * (GLA_DK ** -0.5)
        o, s_new = _gla_chunk(q, k_ref[rows, :].astype(_F32), v_ref[rows, :], la_ref[rows, :],
                              s_scr[...], m_cat, masks, ones_col, c, levels)
        oa_ref[rows, :] = _gla_post(o, r_ref[rows, :].astype(_F32), g).astype(oa_ref.dtype)
        s_scr[...] = s_new
        return carry

    lax.fori_loop(0, n_sub, body, 0)

    @pl.when(n == pl.num_programs(2) - 1)
    def _():
        sout_ref[0, 0] = s_scr[...]


def _gla_prompt(z, la, gla_g, batch, seq, c, rows_per_step):
    m_cat, masks, levels = _gla_consts(c)
    ones_col = jnp.ones((3 * c, LANES), _BF16)
    n_sub = rows_per_step // c
    steps = seq // rows_per_step
    kern = functools.partial(_gla_prompt_kernel, c=c, levels=levels, n_sub=n_sub)
    rb = lambda b, h, n: b * steps + n
    return pl.pallas_call(
        kern,
        grid=(batch, GLA_HEADS, steps),
        in_specs=[
            pl.BlockSpec((rows_per_step, GLA_DK), lambda b, h, n: (rb(b, h, n), Z_Q // GLA_DK + h)),
            pl.BlockSpec((rows_per_step, GLA_DK), lambda b, h, n: (rb(b, h, n), Z_K // GLA_DK + h)),
            pl.BlockSpec((rows_per_step, GLA_DV), lambda b, h, n: (rb(b, h, n), Z_V // GLA_DV + h)),
            pl.BlockSpec((rows_per_step, GLA_DV), lambda b, h, n: (rb(b, h, n), Z_R // GLA_DV + h)),
            pl.BlockSpec((rows_per_step, GLA_DK), lambda b, h, n: (rb(b, h, n), h)),
            pl.BlockSpec((1, GLA_DV), lambda b, h, n: (0, 0)),
            pl.BlockSpec(m_cat.shape, lambda b, h, n: (0, 0)),
            pl.BlockSpec(masks.shape, lambda b, h, n: (0, 0, 0)),
            pl.BlockSpec(ones_col.shape, lambda b, h, n: (0, 0)),
        ],
        out_specs=[
            pl.BlockSpec((rows_per_step, GLA_DV), lambda b, h, n: (rb(b, h, n), h)),
            pl.BlockSpec((1, 1, GLA_DK, GLA_DV), lambda b, h, n: (b, h, 0, 0)),
        ],
        out_shape=[
            jax.ShapeDtypeStruct((batch * seq, V_W), _BF16),
            jax.ShapeDtypeStruct((batch, GLA_HEADS, GLA_DK, GLA_DV), _F32),
        ],
        scratch_shapes=[pltpu.VMEM((GLA_DK, GLA_DV), _F32)],
        compiler_params=_params("arbitrary", "arbitrary", "arbitrary"),
        name="gla_prompt",
    )(z, z, z, z, la, gla_g, m_cat, masks, ones_col)


def _gla_sample_kernel(q_ref, k_ref, v_ref, r_ref, la_ref, s_ref, g_ref, mc_ref, mk_ref, oc_ref,
                       oa_ref, sout_ref, *, c, seq, levels, bb):
    m_cat = mc_ref[...]
    masks = [mk_ref[i] for i in range(levels + 1)]
    ones_col = oc_ref[...]
    g = g_ref[...]
    per = c // seq

    def own(a, e, width):
        return jnp.concatenate([a[e * seq:(e + 1) * seq], jnp.zeros((c - seq, width), a.dtype)], axis=0)

    def body(i, carry):
        rows = pl.ds(pl.multiple_of(i * c, c), c)
        for h in range(GLA_HEADS):
            dk = slice(h * GLA_DK, (h + 1) * GLA_DK)
            dv = slice(h * GLA_DV, (h + 1) * GLA_DV)
            q = q_ref[rows, dk].astype(_F32) * (GLA_DK ** -0.5)
            k = k_ref[rows, dk].astype(_F32)
            v = v_ref[rows, dv].astype(_F32)
            la = la_ref[rows, dk]
            outs = []
            for e in range(per):
                o, s_new = _gla_chunk(own(q, e, GLA_DK), own(k, e, GLA_DK),
                                      own(v, e, GLA_DV).astype(_BF16), own(la, e, GLA_DK),
                                      s_ref[i * per + e, h], m_cat, masks, ones_col, c, levels)
                sout_ref[i * per + e, h] = s_new
                outs.append(o[0:seq])
            o_all = jnp.concatenate(outs, axis=0)
            oa_ref[rows, dv] = _gla_post(o_all, r_ref[rows, dv].astype(_F32), g).astype(oa_ref.dtype)
        return carry

    lax.fori_loop(0, bb // per, body, 0)


def _gla_sample(z, la, state, gla_g, row0, batch, seq, bb):
    c = 16
    assert c % seq == 0 and seq % 8 == 0 and bb % (c // seq) == 0
    m_cat, masks, levels = _gla_consts(c)
    ones_col = jnp.ones((3 * c, LANES), _BF16)
    rows = bb * seq
    rb0 = row0 // rows
    assert row0 % rows == 0
    kern = functools.partial(_gla_sample_kernel, c=c, seq=seq, levels=levels, bb=bb)
    return pl.pallas_call(
        kern,
        grid=(batch // bb,),
        in_specs=[
            pl.BlockSpec((rows, Q_W), lambda i: (rb0 + i, Z_Q // Q_W)),
            pl.BlockSpec((rows, Q_W), lambda i: (rb0 + i, Z_K // Q_W)),
            pl.BlockSpec((rows, V_W), lambda i: (rb0 + i, Z_V // V_W)),
            pl.BlockSpec((rows, V_W), lambda i: (rb0 + i, Z_R // V_W)),
            pl.BlockSpec((rows, Q_W), lambda i: (rb0 + i, 0)),
            pl.BlockSpec((bb, GLA_HEADS, GLA_DK, GLA_DV), lambda i: (i, 0, 0, 0)),
            pl.BlockSpec((1, GLA_DV), lambda i: (0, 0)),
            pl.BlockSpec(m_cat.shape, lambda i: (0, 0)),
            pl.BlockSpec(masks.shape, lambda i: (0, 0, 0)),
            pl.BlockSpec(ones_col.shape, lambda i: (0, 0)),
        ],
        out_specs=[
            pl.BlockSpec((rows, V_W), lambda i: (i, 0)),
            pl.BlockSpec((bb, GLA_HEADS, GLA_DK, GLA_DV), lambda i: (i, 0, 0, 0)),
        ],
        out_shape=[
            jax.ShapeDtypeStruct((batch * seq, V_W), _BF16),
            jax.ShapeDtypeStruct((batch, GLA_HEADS, GLA_DK, GLA_DV), _F32),
        ],
        compiler_params=_params("arbitrary"),
        name="gla_sample",
    )(z, z, z, z, la, state, gla_g, m_cat, masks, ones_col)


def _window_sums(ext_ref, t0, tt, lead):
    outs = []
    for gi, w in enumerate(POOL_WINDOWS):
        lanes = slice(gi * POOL_GROUP_W, (gi + 1) * POOL_GROUP_W)
        acc = ext_ref[lead + (pl.ds(t0, tt), lanes)]
        for j in range(1, w):
            acc = acc + ext_ref[lead + (pl.ds(t0 - j, tt), lanes)]
        outs.append(acc)
    return outs


def _pool_prompt_kernel(u_ref, p_ref, ext_scr, *, tt):
    n = pl.program_id(1)

    @pl.when(n == 0)
    def _():
        ext_scr[0:HALO, :] = jnp.zeros((HALO, POOL_WIDTH), _F32)

    u = u_ref[...]
    ext_scr[HALO:HALO + tt, :] = u
    sums = _window_sums(ext_scr, HALO, tt, ())
    pos = (n * tt + 1 + lax.broadcasted_iota(jnp.int32, (tt, 1), 0)).astype(_F32)
    for gi, w in enumerate(POOL_WINDOWS):
        lanes = slice(gi * POOL_GROUP_W, (gi + 1) * POOL_GROUP_W)
        cnt = jnp.minimum(pos, float(w))
        p_ref[:, lanes] = (sums[gi] / cnt - u[:, lanes]).astype(p_ref.dtype)
    ext_scr[0:HALO, :] = u[tt - HALO:tt, :]


def _pool_prompt(u, batch, seq, tt):
    steps = seq // tt
    return pl.pallas_call(
        functools.partial(_pool_prompt_kernel, tt=tt),
        grid=(batch, steps),
        in_specs=[pl.BlockSpec((tt, POOL_WIDTH), lambda b, n: (b * steps + n, 0))],
        out_specs=pl.BlockSpec((tt, POOL_WIDTH), lambda b, n: (b * steps + n, 0)),
        out_shape=jax.ShapeDtypeStruct((batch * seq, POOL_WIDTH), _BF16),
        scratch_shapes=[pltpu.VMEM((HALO + tt, POOL_WIDTH), _F32)],
        compiler_params=_params("arbitrary", "arbitrary"),
        name="pool_prompt",
    )(u)


def _pool_sample_kernel(u_ref, buf_ref, p_ref, nb_ref, ext_scr, *, bb, seq):
    u = u_ref[...].reshape(bb, seq, POOL_WIDTH)
    ext_scr[:, 0:HALO - POOL_BUF, :] = jnp.zeros((bb, HALO - POOL_BUF, POOL_WIDTH), _F32)
    ext_scr[:, HALO - POOL_BUF:HALO, :] = buf_ref[...]
    ext_scr[:, HALO:HALO + seq, :] = u
    sums = _window_sums(ext_scr, HALO, seq, (slice(None),))
    for gi, w in enumerate(POOL_WINDOWS):
        lanes = slice(gi * POOL_GROUP_W, (gi + 1) * POOL_GROUP_W)
        pg = sums[gi] / float(w) - u[:, :, lanes]
        p_ref[:, lanes] = pg.reshape(bb * seq, POOL_GROUP_W).astype(p_ref.dtype)
    nb_ref[...] = ext_scr[:, HALO + seq - POOL_BUF:HALO + seq, :]


def _pool_sample(u, buf, row0, seq, bb):
    batch = buf.shape[0]
    rows = bb * seq
    rb0 = row0 // rows
    assert PAST_LEN >= max(POOL_WINDOWS) and row0 % rows == 0 and seq == 8
    return pl.pallas_call(
        functools.partial(_pool_sample_kernel, bb=bb, seq=seq),
        grid=(batch // bb,),
        in_specs=[
            pl.BlockSpec((rows, POOL_WIDTH), lambda i: (rb0 + i, 0)),
            pl.BlockSpec((bb, POOL_BUF, POOL_WIDTH), lambda i: (i, 0, 0)),
        ],
        out_specs=[
            pl.BlockSpec((rows, POOL_WIDTH), lambda i: (i, 0)),
            pl.BlockSpec((bb, POOL_BUF, POOL_WIDTH), lambda i: (i, 0, 0)),
        ],
        out_shape=[
            jax.ShapeDtypeStruct((batch * seq, POOL_WIDTH), _BF16),
            jax.ShapeDtypeStruct((batch, POOL_BUF, POOL_WIDTH), _F32),
        ],
        scratch_shapes=[pltpu.VMEM((bb, HALO + seq, POOL_WIDTH), _F32)],
        compiler_params=_params("arbitrary"),
        name="pool_sample",
    )(u, buf)


def _merge_out_kernel(x_ref, oa_ref, p_ref, ga_ref, gb_ref, pw_ref, ps_ref, wo_ref, g2_ref,
                      x1_ref, h2_ref):
    p = p_ref[...]
    ob = jnp.concatenate(
        [_dot(p[:, gi * POOL_GROUP_W:(gi + 1) * POOL_GROUP_W], pw_ref[gi])
         for gi in range(POOL_GROUPS)], axis=1) * ps_ref[...]
    m = (jax.nn.sigmoid(ga_ref[...].astype(_F32)) * oa_ref[...].astype(_F32)
         + jax.nn.sigmoid(gb_ref[...].astype(_F32)) * ob)
    x1 = x_ref[...] + _dot(m.astype(_BF16), wo_ref[...])
    x1_ref[...] = x1
    h2_ref[...] = (x1 * _rms_scale(x1) * g2_ref[...]).astype(_BF16)


def _merge_out(x, oa, p, zg, row0, pool_w, pool_scale, w_out, g2, tm):
    m = x.shape[0]
    rb0 = row0 // tm
    assert row0 % tm == 0
    return pl.pallas_call(
        _merge_out_kernel,
        grid=(m // tm,),
        in_specs=[
            pl.BlockSpec((tm, D_MODEL), lambda i: (i, 0)),
            pl.BlockSpec((tm, D_MODEL), lambda i: (i, 0)),
            pl.BlockSpec((tm, POOL_WIDTH), lambda i: (i, 0)),
            pl.BlockSpec((tm, D_MODEL), lambda i: (rb0 + i, 0)),
            pl.BlockSpec((tm, D_MODEL), lambda i: (rb0 + i, 1)),
            pl.BlockSpec((POOL_GROUPS, POOL_GROUP_W, POOL_OUT_GROUP_W), lambda i: (0, 0, 0)),
            pl.BlockSpec((1, D_MODEL), lambda i: (0, 0)),
            pl.BlockSpec((D_MODEL, D_MODEL), lambda i: (0, 0)),
            pl.BlockSpec((1, D_MODEL), lambda i: (0, 0)),
        ],
        out_specs=[
            pl.BlockSpec((tm, D_MODEL), lambda i: (i, 0)),
            pl.BlockSpec((tm, D_MODEL), lambda i: (i, 0)),
        ],
        out_shape=[
            jax.ShapeDtypeStruct((m, D_MODEL), _F32),
            jax.ShapeDtypeStruct((m, D_MODEL), _BF16),
        ],
        compiler_params=_params("arbitrary"),
        name="merge_out",
    )(x, oa, p, zg, zg, pool_w, pool_scale, w_out, g2)


def _mlp_kernel(h2_ref, x1_ref, wu_ref, wd_ref, g_ref, y_ref):
    f = pl.program_id(1)

    @pl.when(f == 0)
    def _():
        y_ref[...] = x1_ref[...]

    a = jnp.maximum(_dot(h2_ref[...], wu_ref[...]), 0.0)
    y_ref[...] += _dot((a * a).astype(_BF16), wd_ref[...])

    @pl.when(f == pl.num_programs(1) - 1)
    def _():
        x2 = y_ref[...]
        y_ref[...] = x2 * _rms_scale(x2) * g_ref[...]


def _mlp(h2, x1, w_up, w_down, g, tm, tf):
    m = h2.shape[0]
    return pl.pallas_call(
        _mlp_kernel,
        grid=(m // tm, D_FF // tf),
        in_specs=[
            pl.BlockSpec((tm, D_MODEL), lambda i, f: (i, 0)),
            pl.BlockSpec((tm, D_MODEL), lambda i, f: (i, 0)),
            pl.BlockSpec((D_MODEL, tf), lambda i, f: (0, f)),
            pl.BlockSpec((tf, D_MODEL), lambda i, f: (f, 0)),
            pl.BlockSpec((1, D_MODEL), lambda i, f: (0, 0)),
        ],
        out_specs=pl.BlockSpec((tm, D_MODEL), lambda i, f: (i, 0)),
        out_shape=jax.ShapeDtypeStruct((m, D_MODEL), _F32),
        compiler_params=_params("arbitrary", "arbitrary"),
        name="mlp",
    )(h2, x1, w_up, w_down, g)


def kernel(x_prompt, x_sample, state_gla, state_pool, norm_mix_g, w_in, w_alpha_up, b_alpha,
           gla_norm_g, pool_w, pool_scale, w_out, norm_mlp_g, w_up, w_down, norm_final_g):
    batch, seq, _ = x_prompt.shape
    dec_batch, dec_seq, _ = x_sample.shape
    n_p = batch * seq
    n_s = dec_batch * dec_seq
    xp = x_prompt.reshape(n_p, D_MODEL)
    xs = x_sample.reshape(n_s, D_MODEL)
    row = lambda v: v.reshape(1, -1).astype(_F32)

    wa_pad = jnp.pad(w_in[:, A_OFF:U_OFF], ((0, 0), (0, LANES - GATE_RANK))).astype(_BF16)
    wup_pad = jnp.pad(w_alpha_up, ((0, LANES - GATE_RANK), (0, 0))).astype(_BF16)
    wup3 = jnp.concatenate([wup_pad] * 3, axis=0)
    pool_w_b, w_out_b = pool_w.astype(_BF16), w_out.astype(_BF16)
    w_up_b, w_down_b = w_up.astype(_BF16), w_down.astype(_BF16)

    h, la = _norm_gate(xp, xs, row(norm_mix_g), wa_pad, wup3, row(b_alpha), tm=512)
    z = _proj(h, w_in, 0, QKVR_W // PROJ_TN, _BF16, tm=1024)
    zg = _proj(h, w_in, GA_OFF, 2 * D_MODEL // PROJ_TN, _BF16, tm=1024)
    u = _proj(h, w_in, U_OFF, POOL_WIDTH // PROJ_TN, _F32, tm=1024)

    oa_p, s_gla_p = _gla_prompt(z, la, row(gla_norm_g), batch, seq, c=64, rows_per_step=256)
    oa_s, s_gla_s = _gla_sample(z, la, state_gla, row(gla_norm_g), n_p, dec_batch, dec_seq, bb=2)
    p_p = _pool_prompt(u, batch, seq, tt=256)
    p_s, buf_s = _pool_sample(u, state_pool, n_p, dec_seq, bb=16)
    buf_p = u[:n_p].reshape(batch, seq, POOL_WIDTH)[:, seq - POOL_BUF:, :]

    outs = []
    for x_g, oa_g, p_g, row0 in ((xp, oa_p, p_p, 0), (xs, oa_s, p_s, n_p)):
        x1, h2 = _merge_out(x_g, oa_g, p_g, zg, row0, pool_w_b, row(pool_scale), w_out_b,
                            row(norm_mlp_g), tm=256)
        outs.append(_mlp(h2, x1, w_up_b, w_down_b, row(norm_final_g), tm=1024, tf=512))

    y_p = outs[0].reshape(batch, seq, D_MODEL)
    y_s = outs[1].reshape(dec_batch, dec_seq, D_MODEL)
    return (y_p, y_s, s_gla_p, buf_p, s_gla_s, buf_s)
```

```python
import functools

import jax
import jax.numpy as jnp
import numpy as np
from jax import lax
from jax.experimental import pallas as pl
from jax.experimental.pallas import tpu as pltpu

D_MODEL = 2048
GLA_HEADS = 4
GLA_DK = 256
GLA_DV = 512
GATE_RANK = 16
GATE_TAU = 16.0
POOL_WIDTH = 1024
POOL_GROUPS = 4
POOL_GROUP_W = 256
POOL_OUT_GROUP_W = 512
POOL_WINDOWS = (2, 4, 8, 16)
POOL_BUF = 15
D_FF = 8192
EPS = 1e-6
PAST_LEN = 16384

Q_W = GLA_HEADS * GLA_DK
V_W = GLA_HEADS * GLA_DV
QKVR_W = 2 * Q_W + 2 * V_W
A_OFF = QKVR_W
U_OFF = A_OFF + GATE_RANK
GA_OFF = U_OFF + POOL_WIDTH
Z_Q, Z_K, Z_V, Z_R = 0, Q_W, 2 * Q_W, 2 * Q_W + V_W

LANES = 128
PROJ_TN = 1024
HALO = 16
VMEM_LIMIT = 56 * 1024 * 1024

_F32 = jnp.float32
_BF16 = jnp.bfloat16


def _dot(a, b):
    return jnp.dot(a, b, preferred_element_type=_F32)


def _dot_nt(a, b):
    return lax.dot_general(a, b, (((1,), (1,)), ((), ())), preferred_element_type=_F32)


def _dot_tn(a, b):
    return lax.dot_general(a, b, (((0,), (0,)), ((), ())), preferred_element_type=_F32)


def _split3(x):
    hi = x.astype(_BF16)
    r1 = x - hi.astype(_F32)
    mid = r1.astype(_BF16)
    lo = (r1 - mid.astype(_F32)).astype(_BF16)
    return hi, mid, lo


def _rms_scale(x):
    return lax.rsqrt(jnp.mean(x * x, axis=-1, keepdims=True) + EPS)


def _params(*sem):
    return pltpu.CompilerParams(dimension_semantics=sem, vmem_limit_bytes=VMEM_LIMIT)


def _norm_gate_kernel(xp_ref, xs_ref, g_ref, wa_ref, wup_ref, ba_ref, h_ref, la_ref, *, prompt_tiles):
    i = pl.program_id(0)

    def compute(x_ref):
        x = x_ref[...]
        hb = (x * _rms_scale(x) * g_ref[...]).astype(_BF16)
        h_ref[...] = hb
        a_lr = _dot_nt(hb, wa_ref[...].astype(_BF16))
        a_hi, a_mid, a_lo = _split3(a_lr)
        a3 = jnp.concatenate([a_hi, a_mid, a_lo], axis=1)
        a_logit = _dot(a3, wup_ref[...]) + ba_ref[...]
        ls = jnp.minimum(a_logit, 0.0) - jnp.log1p(jnp.exp(-jnp.abs(a_logit)))
        la_ref[...] = ls * (1.0 / GATE_TAU)

    @pl.when(i < prompt_tiles)
    def _():
        compute(xp_ref)

    @pl.when(i >= prompt_tiles)
    def _():
        compute(xs_ref)


def _norm_gate(xp, xs, g, wt, wup3, b_alpha, tm):
    n_p, n_s = xp.shape[0], xs.shape[0]
    pt, st = n_p // tm, n_s // tm
    m = n_p + n_s
    const = lambda i: (0, 0)
    return pl.pallas_call(
        functools.partial(_norm_gate_kernel, prompt_tiles=pt),
        grid=(pt + st,),
        in_specs=[
            pl.BlockSpec((tm, D_MODEL), lambda i: (jnp.minimum(i, pt - 1), 0)),
            pl.BlockSpec((tm, D_MODEL), lambda i: (jnp.maximum(i - pt, 0), 0)),
            pl.BlockSpec((1, D_MODEL), const),
            pl.BlockSpec((LANES, D_MODEL), lambda i: (A_OFF // LANES, 0)),
            pl.BlockSpec((3 * LANES, Q_W), const),
            pl.BlockSpec((1, Q_W), const),
        ],
        out_specs=[
            pl.BlockSpec((tm, D_MODEL), lambda i: (i, 0)),
            pl.BlockSpec((tm, Q_W), lambda i: (i, 0)),
        ],
        out_shape=[
            jax.ShapeDtypeStruct((m, D_MODEL), _BF16),
            jax.ShapeDtypeStruct((m, Q_W), _F32),
        ],
        compiler_params=_params("arbitrary"),
        name="norm_gate",
    )(xp, xs, g, wt, wup3, b_alpha)


def _proj_kernel(h_ref, wt_ref, z_ref, w_scr):
    rc = 256

    @pl.when(pl.program_id(1) == 0)
    def _():
        for r in range(0, w_scr.shape[0], rc):
            w_scr[r:r + rc, :] = wt_ref[r:r + rc, :].astype(_BF16)

    z_ref[...] = _dot_nt(h_ref[...], w_scr[...]).astype(z_ref.dtype)


def _proj(h, wt, col0, n_tiles, out_dtype, tm, tn=PROJ_TN):
    m = h.shape[0]
    assert col0 % 8 == 0 and col0 + n_tiles * tn <= wt.shape[0]
    return pl.pallas_call(
        _proj_kernel,
        grid=(n_tiles, m // tm),
        in_specs=[
            pl.BlockSpec((tm, D_MODEL), lambda j, i: (i, 0)),
            pl.BlockSpec((pl.Element(tn), pl.Element(D_MODEL)), lambda j, i: (pl.multiple_of(col0 + j * tn, 8), 0)),
        ],
        out_specs=pl.BlockSpec((tm, tn), lambda j, i: (i, j)),
        out_shape=jax.ShapeDtypeStruct((m, n_tiles * tn), out_dtype),
        scratch_shapes=[pltpu.VMEM((tn, D_MODEL), _BF16)],
        compiler_params=_params("arbitrary", "arbitrary"),
        name=f"proj_c{col0}",
    )(h, wt)


def _gla_consts(c, per=1):
    g = c // per
    levels = int(np.log2(g))
    assert 2 ** levels == g and g * per == c
    t = np.arange(c)
    seq_id = t // g
    mats = [np.tril(np.ones((c, c), np.float32)) * (seq_id[:, None] == seq_id[None, :])]
    masks = [np.eye(c, dtype=np.float32)]
    for l in range(levels):
        h = 2 ** l
        blk = t // (2 * h)
        mid = blk * 2 * h + h - 1
        upper = (t // h) % 2 == 1
        m = np.zeros((c, c), np.float32)
        for row in range(c):
            if upper[row]:
                m[row, mid[row] + 1: row + 1] = 1.0
            else:
                m[row, row + 1: mid[row] + 1] = 1.0
        mats.append(m)
        masks.append(((blk[:, None] == blk[None, :]) & upper[:, None] & (~upper)[None, :])
                     .astype(np.float32))
    m_all = np.concatenate(mats, axis=0)
    m_cat = np.concatenate([m_all, m_all, m_all], axis=1)
    rowmasks = np.stack([(seq_id == e).astype(np.float32)[:, None] for e in range(per)])
    ones_cols = np.stack([np.tile(np.broadcast_to(rowmasks[e], (c, LANES)), (3, 1))
                          for e in range(per)])
    return (jnp.asarray(m_cat, _BF16), jnp.asarray(np.stack(masks), _F32),
            jnp.asarray(ones_cols, _BF16), jnp.asarray(rowmasks, _F32), levels)


def _gla_chunks(qs, ks, vbs, la_all, states, m_cat, masks, ones_cols, rowmasks, c, levels):
    n = len(qs)
    per = len(states[0])
    g = c // per
    sl = lambda a, i: a[:, i * GLA_DK:(i + 1) * GLA_DK]
    hi, mid, lo = _split3(la_all)
    la3 = jnp.concatenate([hi, mid, lo], axis=0)
    eb = _dot(m_cat, la3)
    dcols = [jnp.exp(_dot_tn(la3, ones_cols[e])) for e in range(per)]
    b = eb[0:c]
    scores = [jnp.where(masks[0] > 0, _dot_nt(qs[i].astype(_BF16), ks[i].astype(_BF16)), 0.0)
              for i in range(n)]
    for l in range(levels):
        x = jnp.exp(eb[(l + 1) * c:(l + 2) * c])
        for i in range(n):
            xi = sl(x, i)
            s_l = _dot_nt((qs[i] * xi).astype(_BF16), (ks[i] * xi).astype(_BF16))
            scores[i] = jnp.where(masks[l + 1] > 0, s_l, scores[i])
    eb0 = jnp.exp(b)
    ebl = jnp.exp(jnp.concatenate(
        [b[(e + 1) * g - 1:(e + 1) * g, :] - b[e * g:(e + 1) * g, :] for e in range(per)], axis=0))
    pick = (lambda a, e: a) if per == 1 else (lambda a, e: a * rowmasks[e])
    outs, new_states = [], []
    for i in range(n):
        o = _dot(scores[i].astype(_BF16), vbs[i])
        qd = qs[i] * sl(eb0, i)
        for e in range(per):
            o = o + _dot(pick(qd, e).astype(_BF16), states[i][e].astype(_BF16))
        outs.append(o)
    for i in range(n):
        kd = ks[i] * sl(ebl, i)
        upd = []
        for e in range(per):
            di = dcols[e][i * GLA_DK:(i + 1) * GLA_DK]
            decay = jnp.concatenate([di] * (GLA_DV // LANES), axis=1)
            upd.append(decay * states[i][e] + _dot_tn(pick(kd, e).astype(_BF16), vbs[i]))
        new_states.append(upd)
    return outs, new_states


def _gla_post(o, r, g):
    o_n = o * _rms_scale(o) * g
    return o_n * (r * jax.nn.sigmoid(r))


def _load_gla_consts(mc_ref, mk_ref, oc_ref, rm_ref):
    return (mc_ref[...], [mk_ref[i] for i in range(mk_ref.shape[0])],
            [oc_ref[e] for e in range(oc_ref.shape[0])], [rm_ref[e] for e in range(rm_ref.shape[0])])


def _const_specs(arrays):
    return [pl.BlockSpec(a.shape, lambda *idx, nd=a.ndim: (0,) * nd) for a in arrays]


def _gla_prompt_kernel(q_ref, k_ref, v_ref, r_ref, la_ref, g_ref, mc_ref, mk_ref, oc_ref, rm_ref,
                       oa_ref, sout_ref, s_scr, *, c, levels, n_sub):
    n = pl.program_id(1)

    @pl.when(n == 0)
    def _():
        s_scr[...] = jnp.zeros_like(s_scr)

    m_cat, masks, ones_cols, rowmasks = _load_gla_consts(mc_ref, mk_ref, oc_ref, rm_ref)
    g = g_ref[...]

    def body(i, carry):
        rows = pl.ds(pl.multiple_of(i * c, c), c)
        dk = lambda h: slice(h * GLA_DK, (h + 1) * GLA_DK)
        dv = lambda h: slice(h * GLA_DV, (h + 1) * GLA_DV)
        heads = range(GLA_HEADS)
        qs = [q_ref[rows, dk(h)].astype(_F32) * (GLA_DK ** -0.5) for h in heads]
        ks = [k_ref[rows, dk(h)].astype(_F32) for h in heads]
        vbs = [v_ref[rows, dv(h)] for h in heads]
        outs, new_states = _gla_chunks(qs, ks, vbs, la_ref[rows, :], [[s_scr[h]] for h in heads],
                                       m_cat, masks, ones_cols, rowmasks, c, levels)
        for h in heads:
            oa_ref[rows, dv(h)] = _gla_post(outs[h], r_ref[rows, dv(h)].astype(_F32), g
                                            ).astype(oa_ref.dtype)
            s_scr[h] = new_states[h][0]
        return carry

    lax.fori_loop(0, n_sub, body, 0)

    @pl.when(n == pl.num_programs(1) - 1)
    def _():
        sout_ref[0] = s_scr[...]


def _gla_prompt(z, la, gla_g, batch, seq, c, rows_per_step):
    *consts, levels = _gla_consts(c)
    n_sub = rows_per_step // c
    steps = seq // rows_per_step
    kern = functools.partial(_gla_prompt_kernel, c=c, levels=levels, n_sub=n_sub)
    rb = lambda b, n: b * steps + n
    return pl.pallas_call(
        kern,
        grid=(batch, steps),
        in_specs=[
            pl.BlockSpec((rows_per_step, Q_W), lambda b, n: (rb(b, n), Z_Q // Q_W)),
            pl.BlockSpec((rows_per_step, Q_W), lambda b, n: (rb(b, n), Z_K // Q_W)),
            pl.BlockSpec((rows_per_step, V_W), lambda b, n: (rb(b, n), Z_V // V_W)),
            pl.BlockSpec((rows_per_step, V_W), lambda b, n: (rb(b, n), Z_R // V_W)),
            pl.BlockSpec((rows_per_step, Q_W), lambda b, n: (rb(b, n), 0)),
            pl.BlockSpec((1, GLA_DV), lambda b, n: (0, 0)),
        ] + _const_specs(consts),
        out_specs=[
            pl.BlockSpec((rows_per_step, V_W), lambda b, n: (rb(b, n), 0)),
            pl.BlockSpec((1, GLA_HEADS, GLA_DK, GLA_DV), lambda b, n: (b, 0, 0, 0)),
        ],
        out_shape=[
            jax.ShapeDtypeStruct((batch * seq, V_W), _BF16),
            jax.ShapeDtypeStruct((batch, GLA_HEADS, GLA_DK, GLA_DV), _F32),
        ],
        scratch_shapes=[pltpu.VMEM((GLA_HEADS, GLA_DK, GLA_DV), _F32)],
        compiler_params=_params("arbitrary", "arbitrary"),
        name="gla_prompt",
    )(z, z, z, z, la, gla_g, *consts)


def _gla_sample_kernel(q_ref, k_ref, v_ref, r_ref, la_ref, s_ref, g_ref, mc_ref, mk_ref, oc_ref,
                       rm_ref, oa_ref, sout_ref, *, c, per, levels, bb):
    m_cat, masks, ones_cols, rowmasks = _load_gla_consts(mc_ref, mk_ref, oc_ref, rm_ref)
    g = g_ref[...]

    def body(i, carry):
        rows = pl.ds(pl.multiple_of(i * c, c), c)
        dk = lambda h: slice(h * GLA_DK, (h + 1) * GLA_DK)
        dv = lambda h: slice(h * GLA_DV, (h + 1) * GLA_DV)
        heads = range(GLA_HEADS)
        qs = [q_ref[rows, dk(h)].astype(_F32) * (GLA_DK ** -0.5) for h in heads]
        ks = [k_ref[rows, dk(h)].astype(_F32) for h in heads]
        vbs = [v_ref[rows, dv(h)] for h in heads]
        states = [[s_ref[i * per + e, h] for e in range(per)] for h in heads]
        outs, new_states = _gla_chunks(qs, ks, vbs, la_ref[rows, :], states,
                                       m_cat, masks, ones_cols, rowmasks, c, levels)
        for h in heads:
            oa_ref[rows, dv(h)] = _gla_post(outs[h], r_ref[rows, dv(h)].astype(_F32), g
                                            ).astype(oa_ref.dtype)
            for e in range(per):
                sout_ref[i * per + e, h] = new_states[h][e]
        return carry

    lax.fori_loop(0, bb // per, body, 0)


def _gla_sample(z, la, state, gla_g, row0, batch, seq, bb):
    c = 16
    per = c // seq
    assert c % seq == 0 and bb % per == 0
    *consts, levels = _gla_consts(c, per)
    rows = bb * seq
    rb0 = row0 // rows
    assert row0 % rows == 0
    kern = functools.partial(_gla_sample_kernel, c=c, per=per, levels=levels, bb=bb)
    return pl.pallas_call(
        kern,
        grid=(batch // bb,),
        in_specs=[
            pl.BlockSpec((rows, Q_W), lambda i: (rb0 + i, Z_Q // Q_W)),
            pl.BlockSpec((rows, Q_W), lambda i: (rb0 + i, Z_K // Q_W)),
            pl.BlockSpec((rows, V_W), lambda i: (rb0 + i, Z_V // V_W)),
            pl.BlockSpec((rows, V_W), lambda i: (rb0 + i, Z_R // V_W)),
            pl.BlockSpec((rows, Q_W), lambda i: (rb0 + i, 0)),
            pl.BlockSpec((bb, GLA_HEADS, GLA_DK, GLA_DV), lambda i: (i, 0, 0, 0)),
            pl.BlockSpec((1, GLA_DV), lambda i: (0, 0)),
        ] + _const_specs(consts),
        out_specs=[
            pl.BlockSpec((rows, V_W), lambda i: (i, 0)),
            pl.BlockSpec((bb, GLA_HEADS, GLA_DK, GLA_DV), lambda i: (i, 0, 0, 0)),
        ],
        out_shape=[
            jax.ShapeDtypeStruct((batch * seq, V_W), _BF16),
            jax.ShapeDtypeStruct((batch, GLA_HEADS, GLA_DK, GLA_DV), _F32),
        ],
        compiler_params=_params("arbitrary"),
        name="gla_sample",
    )(z, z, z, z, la, state, gla_g, *consts)


def _window_sums(ext_ref, t0, tt, lead):
    outs = []
    for gi, w in enumerate(POOL_WINDOWS):
        lanes = slice(gi * POOL_GROUP_W, (gi + 1) * POOL_GROUP_W)
        acc = ext_ref[lead + (pl.ds(t0, tt), lanes)]
        for j in range(1, w):
            acc = acc + ext_ref[lead + (pl.ds(t0 - j, tt), lanes)]
        outs.append(acc)
    return outs


def _pool_prompt_kernel(u_ref, p_ref, ext_scr, *, tt):
    n = pl.program_id(1)

    @pl.when(n == 0)
    def _():
        ext_scr[0:HALO, :] = jnp.zeros((HALO, POOL_WIDTH), _F32)

    u = u_ref[...]
    ext_scr[HALO:HALO + tt, :] = u
    sums = _window_sums(ext_scr, HALO, tt, ())
    pos = (n * tt + 1 + lax.broadcasted_iota(jnp.int32, (tt, 1), 0)).astype(_F32)
    for gi, w in enumerate(POOL_WINDOWS):
        lanes = slice(gi * POOL_GROUP_W, (gi + 1) * POOL_GROUP_W)
        cnt = jnp.minimum(pos, float(w))
        p_ref[:, lanes] = (sums[gi] / cnt - u[:, lanes]).astype(p_ref.dtype)
    ext_scr[0:HALO, :] = u[tt - HALO:tt, :]


def _pool_prompt(u, batch, seq, tt):
    steps = seq // tt
    return pl.pallas_call(
        functools.partial(_pool_prompt_kernel, tt=tt),
        grid=(batch, steps),
        in_specs=[pl.BlockSpec((tt, POOL_WIDTH), lambda b, n: (b * steps + n, 0))],
        out_specs=pl.BlockSpec((tt, POOL_WIDTH), lambda b, n: (b * steps + n, 0)),
        out_shape=jax.ShapeDtypeStruct((batch * seq, POOL_WIDTH), _BF16),
        scratch_shapes=[pltpu.VMEM((HALO + tt, POOL_WIDTH), _F32)],
        compiler_params=_params("arbitrary", "arbitrary"),
        name="pool_prompt",
    )(u)


def _pool_sample_kernel(u_ref, buf_ref, p_ref, nb_ref, ext_scr, *, bb, seq):
    u = u_ref[...].reshape(bb, seq, POOL_WIDTH)
    ext_scr[:, 0:HALO - POOL_BUF, :] = jnp.zeros((bb, HALO - POOL_BUF, POOL_WIDTH), _F32)
    ext_scr[:, HALO - POOL_BUF:HALO, :] = buf_ref[...]
    ext_scr[:, HALO:HALO + seq, :] = u
    sums = _window_sums(ext_scr, HALO, seq, (slice(None),))
    for gi, w in enumerate(POOL_WINDOWS):
        lanes = slice(gi * POOL_GROUP_W, (gi + 1) * POOL_GROUP_W)
        pg = sums[gi] / float(w) - u[:, :, lanes]
        p_ref[:, lanes] = pg.reshape(bb * seq, POOL_GROUP_W).astype(p_ref.dtype)
    nb_ref[...] = ext_scr[:, HALO + seq - POOL_BUF:HALO + seq, :]


def _pool_sample(u, buf, row0, seq, bb):
    batch = buf.shape[0]
    rows = bb * seq
    rb0 = row0 // rows
    assert PAST_LEN >= max(POOL_WINDOWS) and row0 % rows == 0 and seq == 8
    return pl.pallas_call(
        functools.partial(_pool_sample_kernel, bb=bb, seq=seq),
        grid=(batch // bb,),
        in_specs=[
            pl.BlockSpec((rows, POOL_WIDTH), lambda i: (rb0 + i, 0)),
            pl.BlockSpec((bb, POOL_BUF, POOL_WIDTH), lambda i: (i, 0, 0)),
        ],
        out_specs=[
            pl.BlockSpec((rows, POOL_WIDTH), lambda i: (i, 0)),
            pl.BlockSpec((bb, POOL_BUF, POOL_WIDTH), lambda i: (i, 0, 0)),
        ],
        out_shape=[
            jax.ShapeDtypeStruct((batch * seq, POOL_WIDTH), _BF16),
            jax.ShapeDtypeStruct((batch, POOL_BUF, POOL_WIDTH), _F32),
        ],
        scratch_shapes=[pltpu.VMEM((bb, HALO + seq, POOL_WIDTH), _F32)],
        compiler_params=_params("arbitrary"),
        name="pool_sample",
    )(u, buf)


def _merge_out_kernel(x_ref, oa_ref, p_ref, ga_ref, gb_ref, pw_ref, ps_ref, wo_ref, g2_ref,
                      x1_ref, h2_ref):
    p = p_ref[...]
    ob = jnp.concatenate(
        [_dot(p[:, gi * POOL_GROUP_W:(gi + 1) * POOL_GROUP_W], pw_ref[gi])
         for gi in range(POOL_GROUPS)], axis=1) * ps_ref[...]
    m = (jax.nn.sigmoid(ga_ref[...].astype(_F32)) * oa_ref[...].astype(_F32)
         + jax.nn.sigmoid(gb_ref[...].astype(_F32)) * ob)
    x1 = x_ref[...] + _dot(m.astype(_BF16), wo_ref[...])
    x1_ref[...] = x1
    h2_ref[...] = (x1 * _rms_scale(x1) * g2_ref[...]).astype(_BF16)


def _merge_out(x, oa, p, zg, row0, pool_w, pool_scale, w_out, g2, tm):
    m = x.shape[0]
    rb0 = row0 // tm
    assert row0 % tm == 0
    return pl.pallas_call(
        _merge_out_kernel,
        grid=(m // tm,),
        in_specs=[
            pl.BlockSpec((tm, D_MODEL), lambda i: (i, 0)),
            pl.BlockSpec((tm, D_MODEL), lambda i: (i, 0)),
            pl.BlockSpec((tm, POOL_WIDTH), lambda i: (i, 0)),
            pl.BlockSpec((tm, D_MODEL), lambda i: (rb0 + i, 0)),
            pl.BlockSpec((tm, D_MODEL), lambda i: (rb0 + i, 1)),
            pl.BlockSpec((POOL_GROUPS, POOL_GROUP_W, POOL_OUT_GROUP_W), lambda i: (0, 0, 0)),
            pl.BlockSpec((1, D_MODEL), lambda i: (0, 0)),
            pl.BlockSpec((D_MODEL, D_MODEL), lambda i: (0, 0)),
            pl.BlockSpec((1, D_MODEL), lambda i: (0, 0)),
        ],
        out_specs=[
            pl.BlockSpec((tm, D_MODEL), lambda i: (i, 0)),
            pl.BlockSpec((tm, D_MODEL), lambda i: (i, 0)),
        ],
        out_shape=[
            jax.ShapeDtypeStruct((m, D_MODEL), _F32),
            jax.ShapeDtypeStruct((m, D_MODEL), _BF16),
        ],
        compiler_params=_params("arbitrary"),
        name="merge_out",
    )(x, oa, p, zg, zg, pool_w, pool_scale, w_out, g2)


def _mlp_kernel(h2_ref, x1_ref, wu_ref, wd_ref, g_ref, y_ref):
    f = pl.program_id(1)

    @pl.when(f == 0)
    def _():
        y_ref[...] = x1_ref[...]

    a = jnp.maximum(_dot(h2_ref[...], wu_ref[...]), 0.0)
    y_ref[...] += _dot((a * a).astype(_BF16), wd_ref[...])

    @pl.when(f == pl.num_programs(1) - 1)
    def _():
        x2 = y_ref[...]
        y_ref[...] = x2 * _rms_scale(x2) * g_ref[...]


def _mlp(h2, x1, w_up, w_down, g, tm, tf):
    m = h2.shape[0]
    return pl.pallas_call(
        _mlp_kernel,
        grid=(m // tm, D_FF // tf),
        in_specs=[
            pl.BlockSpec((tm, D_MODEL), lambda i, f: (i, 0)),
            pl.BlockSpec((tm, D_MODEL), lambda i, f: (i, 0)),
            pl.BlockSpec((D_MODEL, tf), lambda i, f: (0, f)),
            pl.BlockSpec((tf, D_MODEL), lambda i, f: (f, 0)),
            pl.BlockSpec((1, D_MODEL), lambda i, f: (0, 0)),
        ],
        out_specs=pl.BlockSpec((tm, D_MODEL), lambda i, f: (i, 0)),
        out_shape=jax.ShapeDtypeStruct((m, D_MODEL), _F32),
        compiler_params=_params("arbitrary", "arbitrary"),
        name="mlp",
    )(h2, x1, w_up, w_down, g)


def kernel(x_prompt, x_sample, state_gla, state_pool, norm_mix_g, w_in, w_alpha_up, b_alpha,
           gla_norm_g, pool_w, pool_scale, w_out, norm_mlp_g, w_up, w_down, norm_final_g):
    batch, seq, _ = x_prompt.shape
    dec_batch, dec_seq, _ = x_sample.shape
    n_p = batch * seq
    n_s = dec_batch * dec_seq
    xp = x_prompt.reshape(n_p, D_MODEL)
    xs = x_sample.reshape(n_s, D_MODEL)
    row = lambda v: v.reshape(1, -1).astype(_F32)

    wup_pad = jnp.pad(w_alpha_up, ((0, LANES - GATE_RANK), (0, 0))).astype(_BF16)
    wup3 = jnp.concatenate([wup_pad] * 3, axis=0)
    pool_w_b, w_out_b = pool_w.astype(_BF16), w_out.astype(_BF16)
    w_up_b, w_down_b = w_up.astype(_BF16), w_down.astype(_BF16)
    wt = w_in.T

    h, la = _norm_gate(xp, xs, row(norm_mix_g), wt, wup3, row(b_alpha), tm=512)
    z = _proj(h, wt, 0, QKVR_W // PROJ_TN, _BF16, tm=1024)
    zg = _proj(h, wt, GA_OFF, 2 * D_MODEL // PROJ_TN, _BF16, tm=1024)
    u = _proj(h, wt, U_OFF, POOL_WIDTH // PROJ_TN, _F32, tm=1024)

    oa_p, s_gla_p = _gla_prompt(z, la, row(gla_norm_g), batch, seq, c=64, rows_per_step=512)
    oa_s, s_gla_s = _gla_sample(z, la, state_gla, row(gla_norm_g), n_p, dec_batch, dec_seq, bb=2)
    p_p = _pool_prompt(u, batch, seq, tt=256)
    p_s, buf_s = _pool_sample(u, state_pool, n_p, dec_seq, bb=16)
    buf_p = jnp.stack([u[(b + 1) * seq - POOL_BUF:(b + 1) * seq] for b in range(batch)], axis=0)

    outs = []
    for x_g, oa_g, p_g, row0 in ((xp, oa_p, p_p, 0), (xs, oa_s, p_s, n_p)):
        x1, h2 = _merge_out(x_g, oa_g, p_g, zg, row0, pool_w_b, row(pool_scale), w_out_b,
                            row(norm_mlp_g), tm=256)
        outs.append(_mlp(h2, x1, w_up_b, w_down_b, row(norm_final_g), tm=1024, tf=512))

    y_p = outs[0].reshape(batch, seq, D_MODEL)
    y_s = outs[1].reshape(dec_batch, dec_seq, D_MODEL)
    return (y_p, y_s, s_gla_p, buf_p, s_gla_s, buf_s)
```

```python
import functools

import jax
import jax.numpy as jnp
import numpy as np
from jax import lax
from jax.experimental import pallas as pl
from jax.experimental.pallas import tpu as pltpu

D_MODEL = 2048
GLA_HEADS = 4
GLA_DK = 256
GLA_DV = 512
GATE_RANK = 16
GATE_TAU = 16.0
POOL_WIDTH = 1024
POOL_GROUPS = 4
POOL_GROUP_W = 256
POOL_OUT_GROUP_W = 512
POOL_WINDOWS = (2, 4, 8, 16)
POOL_BUF = 15
D_FF = 8192
EPS = 1e-6
PAST_LEN = 16384

Q_W = GLA_HEADS * GLA_DK
V_W = GLA_HEADS * GLA_DV
QKVR_W = 2 * Q_W + 2 * V_W
A_OFF = QKVR_W
U_OFF = A_OFF + GATE_RANK
GA_OFF = U_OFF + POOL_WIDTH
Z_Q, Z_K, Z_V, Z_R = 0, Q_W, 2 * Q_W, 2 * Q_W + V_W

LANES = 128
PROJ_TN = 1024
HALO = 16
VMEM_LIMIT = 56 * 1024 * 1024

_F32 = jnp.float32
_BF16 = jnp.bfloat16


def _dot(a, b):
    return jnp.dot(a, b, preferred_element_type=_F32)


def _dot_nt(a, b):
    return lax.dot_general(a, b, (((1,), (1,)), ((), ())), preferred_element_type=_F32)


def _dot_tn(a, b):
    return lax.dot_general(a, b, (((0,), (0,)), ((), ())), preferred_element_type=_F32)


def _split3(x):
    hi = x.astype(_BF16)
    r1 = x - hi.astype(_F32)
    mid = r1.astype(_BF16)
    lo = (r1 - mid.astype(_F32)).astype(_BF16)
    return hi, mid, lo


def _rms_scale(x):
    return lax.rsqrt(jnp.mean(x * x, axis=-1, keepdims=True) + EPS)


def _params(*sem):
    return pltpu.CompilerParams(dimension_semantics=sem, vmem_limit_bytes=VMEM_LIMIT)


def _norm_gate_u_kernel(xp_ref, xs_ref, g_ref, wa_ref, wup_ref, ba_ref, wu_ref,
                        h_ref, la_ref, u_ref, wu_scr, *, prompt_tiles):
    i = pl.program_id(0)
    rc = 256

    @pl.when(i == 0)
    def _():
        for r in range(0, POOL_WIDTH, rc):
            wu_scr[r:r + rc, :] = wu_ref[r:r + rc, :].astype(_BF16)

    def compute(x_ref):
        x = x_ref[...]
        hb = (x * _rms_scale(x) * g_ref[...]).astype(_BF16)
        h_ref[...] = hb
        u_ref[...] = _dot_nt(hb, wu_scr[...])
        a_lr = _dot_nt(hb, wa_ref[...].astype(_BF16))
        a_hi, a_mid, a_lo = _split3(a_lr)
        a3 = jnp.concatenate([a_hi, a_mid, a_lo], axis=1)
        a_logit = _dot(a3, wup_ref[...]) + ba_ref[...]
        ls = jnp.minimum(a_logit, 0.0) - jnp.log1p(jnp.exp(-jnp.abs(a_logit)))
        la_ref[...] = ls * (1.0 / GATE_TAU)

    @pl.when(i < prompt_tiles)
    def _():
        compute(xp_ref)

    @pl.when(i >= prompt_tiles)
    def _():
        compute(xs_ref)


def _norm_gate_u(xp, xs, g, wt, wup3, b_alpha, tm):
    n_p, n_s = xp.shape[0], xs.shape[0]
    pt, st = n_p // tm, n_s // tm
    m = n_p + n_s
    const = lambda i: (0, 0)
    return pl.pallas_call(
        functools.partial(_norm_gate_u_kernel, prompt_tiles=pt),
        grid=(pt + st,),
        in_specs=[
            pl.BlockSpec((tm, D_MODEL), lambda i: (jnp.minimum(i, pt - 1), 0)),
            pl.BlockSpec((tm, D_MODEL), lambda i: (jnp.maximum(i - pt, 0), 0)),
            pl.BlockSpec((1, D_MODEL), const),
            pl.BlockSpec((LANES, D_MODEL), lambda i: (A_OFF // LANES, 0)),
            pl.BlockSpec((3 * LANES, Q_W), const),
            pl.BlockSpec((1, Q_W), const),
            pl.BlockSpec((pl.Element(POOL_WIDTH), pl.Element(D_MODEL)), lambda i: (U_OFF, 0),
                         pipeline_mode=pl.Buffered(1)),
        ],
        out_specs=[
            pl.BlockSpec((tm, D_MODEL), lambda i: (i, 0)),
            pl.BlockSpec((tm, Q_W), lambda i: (i, 0)),
            pl.BlockSpec((tm, POOL_WIDTH), lambda i: (i, 0)),
        ],
        out_shape=[
            jax.ShapeDtypeStruct((m, D_MODEL), _BF16),
            jax.ShapeDtypeStruct((m, Q_W), _F32),
            jax.ShapeDtypeStruct((m, POOL_WIDTH), _F32),
        ],
        scratch_shapes=[pltpu.VMEM((POOL_WIDTH, D_MODEL), _BF16)],
        compiler_params=_params("arbitrary"),
        name="norm_gate_u",
    )(xp, xs, g, wt, wup3, b_alpha, wt)


def _proj_kernel(h_ref, wt_ref, z_ref, w_scr):
    rc = 256

    @pl.when(pl.program_id(1) == 0)
    def _():
        for r in range(0, w_scr.shape[0], rc):
            w_scr[r:r + rc, :] = wt_ref[r:r + rc, :].astype(_BF16)

    z_ref[...] = _dot_nt(h_ref[...], w_scr[...]).astype(z_ref.dtype)


def _proj(h, wt, col0, n_tiles, out_dtype, tm, tn=PROJ_TN):
    m = h.shape[0]
    assert col0 % 8 == 0 and col0 + n_tiles * tn <= wt.shape[0]
    return pl.pallas_call(
        _proj_kernel,
        grid=(n_tiles, m // tm),
        in_specs=[
            pl.BlockSpec((tm, D_MODEL), lambda j, i: (i, 0)),
            pl.BlockSpec((pl.Element(tn), pl.Element(D_MODEL)), lambda j, i: (pl.multiple_of(col0 + j * tn, 8), 0)),
        ],
        out_specs=pl.BlockSpec((tm, tn), lambda j, i: (i, j)),
        out_shape=jax.ShapeDtypeStruct((m, n_tiles * tn), out_dtype),
        scratch_shapes=[pltpu.VMEM((tn, D_MODEL), _BF16)],
        compiler_params=_params("arbitrary", "arbitrary"),
        name=f"proj_c{col0}",
    )(h, wt)


def _gla_consts(c, per=1):
    g = c // per
    levels = int(np.log2(g))
    assert 2 ** levels == g and g * per == c
    t = np.arange(c)
    seq_id = t // g
    mats = [np.tril(np.ones((c, c), np.float32)) * (seq_id[:, None] == seq_id[None, :])]
    masks = [np.eye(c, dtype=np.float32)]
    for l in range(levels):
        h = 2 ** l
        blk = t // (2 * h)
        mid = blk * 2 * h + h - 1
        upper = (t // h) % 2 == 1
        m = np.zeros((c, c), np.float32)
        for row in range(c):
            if upper[row]:
                m[row, mid[row] + 1: row + 1] = 1.0
            else:
                m[row, row + 1: mid[row] + 1] = 1.0
        mats.append(m)
        masks.append(((blk[:, None] == blk[None, :]) & upper[:, None] & (~upper)[None, :])
                     .astype(np.float32))
    m_all = np.concatenate(mats, axis=0)
    m_cat = np.concatenate([m_all, m_all, m_all], axis=1)
    rowmasks = np.stack([(seq_id == e).astype(np.float32)[:, None] for e in range(per)])
    ones_cols = np.stack([np.tile(np.broadcast_to(rowmasks[e], (c, LANES)), (3, 1))
                          for e in range(per)])
    return (jnp.asarray(m_cat, _BF16), jnp.asarray(np.stack(masks), _F32),
            jnp.asarray(ones_cols, _BF16), jnp.asarray(rowmasks, _F32), levels)


def _gla_chunks(qs, ks, vbs, la_all, states, m_cat, masks, ones_cols, rowmasks, c, levels):
    n = len(qs)
    per = len(states[0])
    g = c // per
    sl = lambda a, i: a[:, i * GLA_DK:(i + 1) * GLA_DK]
    hi, mid, lo = _split3(la_all)
    la3 = jnp.concatenate([hi, mid, lo], axis=0)
    eb = _dot(m_cat, la3)
    dcols = [jnp.exp(_dot_tn(la3, ones_cols[e])) for e in range(per)]
    b = eb[0:c]
    scores = [jnp.where(masks[0] > 0, _dot_nt(qs[i].astype(_BF16), ks[i].astype(_BF16)), 0.0)
              for i in range(n)]
    for l in range(levels):
        x = jnp.exp(eb[(l + 1) * c:(l + 2) * c])
        for i in range(n):
            xi = sl(x, i)
            s_l = _dot_nt((qs[i] * xi).astype(_BF16), (ks[i] * xi).astype(_BF16))
            scores[i] = jnp.where(masks[l + 1] > 0, s_l, scores[i])
    eb0 = jnp.exp(b)
    ebl = jnp.exp(jnp.concatenate(
        [b[(e + 1) * g - 1:(e + 1) * g, :] - b[e * g:(e + 1) * g, :] for e in range(per)], axis=0))
    pick = (lambda a, e: a) if per == 1 else (lambda a, e: a * rowmasks[e])
    outs, new_states = [], []
    for i in range(n):
        o = _dot(scores[i].astype(_BF16), vbs[i])
        qd = qs[i] * sl(eb0, i)
        for e in range(per):
            o = o + _dot(pick(qd, e).astype(_BF16), states[i][e].astype(_BF16))
        outs.append(o)
    for i in range(n):
        kd = ks[i] * sl(ebl, i)
        upd = []
        for e in range(per):
            di = dcols[e][i * GLA_DK:(i + 1) * GLA_DK]
            decay = jnp.concatenate([di] * (GLA_DV // LANES), axis=1)
            upd.append(decay * states[i][e] + _dot_tn(pick(kd, e).astype(_BF16), vbs[i]))
        new_states.append(upd)
    return outs, new_states


def _gla_post(o, r, g):
    o_n = o * _rms_scale(o) * g
    return o_n * (r * jax.nn.sigmoid(r))


def _load_gla_consts(mc_ref, mk_ref, oc_ref, rm_ref):
    return (mc_ref[...], [mk_ref[i] for i in range(mk_ref.shape[0])],
            [oc_ref[e] for e in range(oc_ref.shape[0])], [rm_ref[e] for e in range(rm_ref.shape[0])])


def _const_specs(arrays):
    return [pl.BlockSpec(a.shape, lambda *idx, nd=a.ndim: (0,) * nd) for a in arrays]


def _gla_prompt_kernel(q_ref, k_ref, v_ref, r_ref, la_ref, g_ref, mc_ref, mk_ref, oc_ref, rm_ref,
                       oa_ref, sout_ref, s_scr, *, c, levels, n_sub):
    n = pl.program_id(1)

    @pl.when(n == 0)
    def _():
        s_scr[...] = jnp.zeros_like(s_scr)

    m_cat, masks, ones_cols, rowmasks = _load_gla_consts(mc_ref, mk_ref, oc_ref, rm_ref)
    g = g_ref[...]

    def body(i, carry):
        rows = pl.ds(pl.multiple_of(i * c, c), c)
        dk = lambda h: slice(h * GLA_DK, (h + 1) * GLA_DK)
        dv = lambda h: slice(h * GLA_DV, (h + 1) * GLA_DV)
        heads = range(GLA_HEADS)
        qs = [q_ref[rows, dk(h)].astype(_F32) * (GLA_DK ** -0.5) for h in heads]
        ks = [k_ref[rows, dk(h)].astype(_F32) for h in heads]
        vbs = [v_ref[rows, dv(h)] for h in heads]
        outs, new_states = _gla_chunks(qs, ks, vbs, la_ref[rows, :], [[s_scr[h]] for h in heads],
                                       m_cat, masks, ones_cols, rowmasks, c, levels)
        for h in heads:
            oa_ref[rows, dv(h)] = _gla_post(outs[h], r_ref[rows, dv(h)].astype(_F32), g
                                            ).astype(oa_ref.dtype)
            s_scr[h] = new_states[h][0]
        return carry

    lax.fori_loop(0, n_sub, body, 0)

    @pl.when(n == pl.num_programs(1) - 1)
    def _():
        sout_ref[0] = s_scr[...]


def _gla_prompt(z, la, gla_g, batch, seq, c, rows_per_step):
    *consts, levels = _gla_consts(c)
    n_sub = rows_per_step // c
    steps = seq // rows_per_step
    kern = functools.partial(_gla_prompt_kernel, c=c, levels=levels, n_sub=n_sub)
    rb = lambda b, n: b * steps + n
    return pl.pallas_call(
        kern,
        grid=(batch, steps),
        in_specs=[
            pl.BlockSpec((rows_per_step, Q_W), lambda b, n: (rb(b, n), Z_Q // Q_W)),
            pl.BlockSpec((rows_per_step, Q_W), lambda b, n: (rb(b, n), Z_K // Q_W)),
            pl.BlockSpec((rows_per_step, V_W), lambda b, n: (rb(b, n), Z_V // V_W)),
            pl.BlockSpec((rows_per_step, V_W), lambda b, n: (rb(b, n), Z_R // V_W)),
            pl.BlockSpec((rows_per_step, Q_W), lambda b, n: (rb(b, n), 0)),
            pl.BlockSpec((1, GLA_DV), lambda b, n: (0, 0)),
        ] + _const_specs(consts),
        out_specs=[
            pl.BlockSpec((rows_per_step, V_W), lambda b, n: (rb(b, n), 0)),
            pl.BlockSpec((1, GLA_HEADS, GLA_DK, GLA_DV), lambda b, n: (b, 0, 0, 0)),
        ],
        out_shape=[
            jax.ShapeDtypeStruct((batch * seq, V_W), _BF16),
            jax.ShapeDtypeStruct((batch, GLA_HEADS, GLA_DK, GLA_DV), _F32),
        ],
        scratch_shapes=[pltpu.VMEM((GLA_HEADS, GLA_DK, GLA_DV), _F32)],
        compiler_params=_params("arbitrary", "arbitrary"),
        name="gla_prompt",
    )(z, z, z, z, la, gla_g, *consts)


def _gla_sample_kernel(q_ref, k_ref, v_ref, r_ref, la_ref, s_ref, g_ref, mc_ref, mk_ref, oc_ref,
                       rm_ref, oa_ref, sout_ref, *, c, per, levels, bb):
    m_cat, masks, ones_cols, rowmasks = _load_gla_consts(mc_ref, mk_ref, oc_ref, rm_ref)
    g = g_ref[...]

    def body(i, carry):
        rows = pl.ds(pl.multiple_of(i * c, c), c)
        dk = lambda h: slice(h * GLA_DK, (h + 1) * GLA_DK)
        dv = lambda h: slice(h * GLA_DV, (h + 1) * GLA_DV)
        heads = range(GLA_HEADS)
        qs = [q_ref[rows, dk(h)].astype(_F32) * (GLA_DK ** -0.5) for h in heads]
        ks = [k_ref[rows, dk(h)].astype(_F32) for h in heads]
        vbs = [v_ref[rows, dv(h)] for h in heads]
        states = [[s_ref[i * per + e, h] for e in range(per)] for h in heads]
        outs, new_states = _gla_chunks(qs, ks, vbs, la_ref[rows, :], states,
                                       m_cat, masks, ones_cols, rowmasks, c, levels)
        for h in heads:
            oa_ref[rows, dv(h)] = _gla_post(outs[h], r_ref[rows, dv(h)].astype(_F32), g
                                            ).astype(oa_ref.dtype)
            for e in range(per):
                sout_ref[i * per + e, h] = new_states[h][e]
        return carry

    lax.fori_loop(0, bb // per, body, 0)


def _gla_sample(z, la, state, gla_g, row0, batch, seq, bb):
    c = 16
    per = c // seq
    assert c % seq == 0 and bb % per == 0
    *consts, levels = _gla_consts(c, per)
    rows = bb * seq
    rb0 = row0 // rows
    assert row0 % rows == 0
    kern = functools.partial(_gla_sample_kernel, c=c, per=per, levels=levels, bb=bb)
    return pl.pallas_call(
        kern,
        grid=(batch // bb,),
        in_specs=[
            pl.BlockSpec((rows, Q_W), lambda i: (rb0 + i, Z_Q // Q_W)),
            pl.BlockSpec((rows, Q_W), lambda i: (rb0 + i, Z_K // Q_W)),
            pl.BlockSpec((rows, V_W), lambda i: (rb0 + i, Z_V // V_W)),
            pl.BlockSpec((rows, V_W), lambda i: (rb0 + i, Z_R // V_W)),
            pl.BlockSpec((rows, Q_W), lambda i: (rb0 + i, 0)),
            pl.BlockSpec((bb, GLA_HEADS, GLA_DK, GLA_DV), lambda i: (i, 0, 0, 0)),
            pl.BlockSpec((1, GLA_DV), lambda i: (0, 0)),
        ] + _const_specs(consts),
        out_specs=[
            pl.BlockSpec((rows, V_W), lambda i: (i, 0)),
            pl.BlockSpec((bb, GLA_HEADS, GLA_DK, GLA_DV), lambda i: (i, 0, 0, 0)),
        ],
        out_shape=[
            jax.ShapeDtypeStruct((batch * seq, V_W), _BF16),
            jax.ShapeDtypeStruct((batch, GLA_HEADS, GLA_DK, GLA_DV), _F32),
        ],
        compiler_params=_params("arbitrary"),
        name="gla_sample",
    )(z, z, z, z, la, state, gla_g, *consts)


def _window_sums(ext_ref, t0, tt, lead):
    outs = []
    for gi, w in enumerate(POOL_WINDOWS):
        lanes = slice(gi * POOL_GROUP_W, (gi + 1) * POOL_GROUP_W)
        acc = ext_ref[lead + (pl.ds(t0, tt), lanes)]
        for j in range(1, w):
            acc = acc + ext_ref[lead + (pl.ds(t0 - j, tt), lanes)]
        outs.append(acc)
    return outs


def _pool_prompt_tile(u_ref, ext_scr, n, tt):
    @pl.when(n == 0)
    def _():
        ext_scr[0:HALO, :] = jnp.zeros((HALO, POOL_WIDTH), _F32)

    u = u_ref[...]
    ext_scr[HALO:HALO + tt, :] = u
    sums = _window_sums(ext_scr, HALO, tt, ())
    pos = (n * tt + 1 + lax.broadcasted_iota(jnp.int32, (tt, 1), 0)).astype(_F32)
    ps = []
    for gi, w in enumerate(POOL_WINDOWS):
        lanes = slice(gi * POOL_GROUP_W, (gi + 1) * POOL_GROUP_W)
        ps.append(sums[gi] / jnp.minimum(pos, float(w)) - u[:, lanes])
    ext_scr[0:HALO, :] = u[tt - HALO:tt, :]
    return ps


def _pool_sample_tile(u_ref, buf_ref, nb_ref, ext_scr, bb, seq):
    u = u_ref[...].reshape(bb, seq, POOL_WIDTH)
    ext_scr[:, 0:HALO - POOL_BUF, :] = jnp.zeros((bb, HALO - POOL_BUF, POOL_WIDTH), _F32)
    ext_scr[:, HALO - POOL_BUF:HALO, :] = buf_ref[...]
    ext_scr[:, HALO:HALO + seq, :] = u
    sums = _window_sums(ext_scr, HALO, seq, (slice(None),))
    ps = []
    for gi, w in enumerate(POOL_WINDOWS):
        lanes = slice(gi * POOL_GROUP_W, (gi + 1) * POOL_GROUP_W)
        ps.append((sums[gi] / float(w) - u[:, :, lanes]).reshape(bb * seq, POOL_GROUP_W))
    nb_ref[...] = ext_scr[:, HALO + seq - POOL_BUF:HALO + seq, :]
    return ps


def _merge_core(x_ref, oa_ref, ps, ga_ref, gb_ref, pw_ref, ps_ref, wo_ref, g2_ref, x1_ref, h2_ref):
    ob = jnp.concatenate([_dot(ps[gi].astype(_BF16), pw_ref[gi]) for gi in range(POOL_GROUPS)],
                         axis=1) * ps_ref[...]
    m = (jax.nn.sigmoid(ga_ref[...].astype(_F32)) * oa_ref[...].astype(_F32)
         + jax.nn.sigmoid(gb_ref[...].astype(_F32)) * ob)
    x1 = x_ref[...] + _dot(m.astype(_BF16), wo_ref[...])
    x1_ref[...] = x1
    h2_ref[...] = (x1 * _rms_scale(x1) * g2_ref[...]).astype(_BF16)


def _merge_prompt_kernel(x_ref, oa_ref, u_ref, ga_ref, gb_ref, pw_ref, ps_ref, wo_ref, g2_ref,
                         x1_ref, h2_ref, ext_scr, *, tt):
    ps = _pool_prompt_tile(u_ref, ext_scr, pl.program_id(1), tt)
    _merge_core(x_ref, oa_ref, ps, ga_ref, gb_ref, pw_ref, ps_ref, wo_ref, g2_ref, x1_ref, h2_ref)


def _merge_sample_kernel(x_ref, oa_ref, u_ref, buf_ref, ga_ref, gb_ref, pw_ref, ps_ref, wo_ref,
                         g2_ref, x1_ref, h2_ref, nb_ref, ext_scr, *, bb, seq):
    ps = _pool_sample_tile(u_ref, buf_ref, nb_ref, ext_scr, bb, seq)
    _merge_core(x_ref, oa_ref, ps, ga_ref, gb_ref, pw_ref, ps_ref, wo_ref, g2_ref, x1_ref, h2_ref)


def _weight_specs():
    zeros = lambda nd: (lambda *idx: (0,) * nd)
    once = pl.Buffered(1)
    return [
        pl.BlockSpec((POOL_GROUPS, POOL_GROUP_W, POOL_OUT_GROUP_W), zeros(3), pipeline_mode=once),
        pl.BlockSpec((1, D_MODEL), zeros(2)),
        pl.BlockSpec((D_MODEL, D_MODEL), zeros(2), pipeline_mode=once),
        pl.BlockSpec((1, D_MODEL), zeros(2)),
    ]


def _merge_prompt(x, oa, u, zg, pool_w, pool_scale, w_out, g2, batch, seq, tt):
    steps = seq // tt
    rb = lambda b, n: b * steps + n
    return pl.pallas_call(
        functools.partial(_merge_prompt_kernel, tt=tt),
        grid=(batch, steps),
        in_specs=[
            pl.BlockSpec((tt, D_MODEL), lambda b, n: (rb(b, n), 0)),
            pl.BlockSpec((tt, D_MODEL), lambda b, n: (rb(b, n), 0)),
            pl.BlockSpec((tt, POOL_WIDTH), lambda b, n: (rb(b, n), 0)),
            pl.BlockSpec((tt, D_MODEL), lambda b, n: (rb(b, n), 0)),
            pl.BlockSpec((tt, D_MODEL), lambda b, n: (rb(b, n), 1)),
        ] + _weight_specs(),
        out_specs=[
            pl.BlockSpec((tt, D_MODEL), lambda b, n: (rb(b, n), 0)),
            pl.BlockSpec((tt, D_MODEL), lambda b, n: (rb(b, n), 0)),
        ],
        out_shape=[
            jax.ShapeDtypeStruct((batch * seq, D_MODEL), _F32),
            jax.ShapeDtypeStruct((batch * seq, D_MODEL), _BF16),
        ],
        scratch_shapes=[pltpu.VMEM((HALO + tt, POOL_WIDTH), _F32)],
        compiler_params=_params("arbitrary", "arbitrary"),
        name="merge_prompt",
    )(x, oa, u, zg, zg, pool_w, pool_scale, w_out, g2)


def _merge_sample(x, oa, u, buf, zg, row0, pool_w, pool_scale, w_out, g2, seq, bb):
    batch = buf.shape[0]
    rows = bb * seq
    rb0 = row0 // rows
    assert PAST_LEN >= max(POOL_WINDOWS) and row0 % rows == 0 and seq == 8
    return pl.pallas_call(
        functools.partial(_merge_sample_kernel, bb=bb, seq=seq),
        grid=(batch // bb,),
        in_specs=[
            pl.BlockSpec((rows, D_MODEL), lambda i: (i, 0)),
            pl.BlockSpec((rows, D_MODEL), lambda i: (i, 0)),
            pl.BlockSpec((rows, POOL_WIDTH), lambda i: (rb0 + i, 0)),
            pl.BlockSpec((bb, POOL_BUF, POOL_WIDTH), lambda i: (i, 0, 0)),
            pl.BlockSpec((rows, D_MODEL), lambda i: (rb0 + i, 0)),
            pl.BlockSpec((rows, D_MODEL), lambda i: (rb0 + i, 1)),
        ] + _weight_specs(),
        out_specs=[
            pl.BlockSpec((rows, D_MODEL), lambda i: (i, 0)),
            pl.BlockSpec((rows, D_MODEL), lambda i: (i, 0)),
            pl.BlockSpec((bb, POOL_BUF, POOL_WIDTH), lambda i: (i, 0, 0)),
        ],
        out_shape=[
            jax.ShapeDtypeStruct((batch * seq, D_MODEL), _F32),
            jax.ShapeDtypeStruct((batch * seq, D_MODEL), _BF16),
            jax.ShapeDtypeStruct((batch, POOL_BUF, POOL_WIDTH), _F32),
        ],
        scratch_shapes=[pltpu.VMEM((bb, HALO + seq, POOL_WIDTH), _F32)],
        compiler_params=_params("arbitrary"),
        name="merge_sample",
    )(x, oa, u, buf, zg, zg, pool_w, pool_scale, w_out, g2)


def _mlp_kernel(h2_ref, x1_ref, wu_ref, wd_ref, g_ref, y_ref):
    f = pl.program_id(1)

    @pl.when(f == 0)
    def _():
        y_ref[...] = x1_ref[...]

    a = jnp.maximum(_dot(h2_ref[...], wu_ref[...]), 0.0)
    y_ref[...] += _dot((a * a).astype(_BF16), wd_ref[...])

    @pl.when(f == pl.num_programs(1) - 1)
    def _():
        x2 = y_ref[...]
        y_ref[...] = x2 * _rms_scale(x2) * g_ref[...]


def _mlp(h2, x1, w_up, w_down, g, tm, tf):
    m = h2.shape[0]
    return pl.pallas_call(
        _mlp_kernel,
        grid=(m // tm, D_FF // tf),
        in_specs=[
            pl.BlockSpec((tm, D_MODEL), lambda i, f: (i, 0)),
            pl.BlockSpec((tm, D_MODEL), lambda i, f: (i, 0)),
            pl.BlockSpec((D_MODEL, tf), lambda i, f: (0, f)),
            pl.BlockSpec((tf, D_MODEL), lambda i, f: (f, 0)),
            pl.BlockSpec((1, D_MODEL), lambda i, f: (0, 0)),
        ],
        out_specs=pl.BlockSpec((tm, D_MODEL), lambda i, f: (i, 0)),
        out_shape=jax.ShapeDtypeStruct((m, D_MODEL), _F32),
        compiler_params=_params("arbitrary", "arbitrary"),
        name="mlp",
    )(h2, x1, w_up, w_down, g)


def kernel(x_prompt, x_sample, state_gla, state_pool, norm_mix_g, w_in, w_alpha_up, b_alpha,
           gla_norm_g, pool_w, pool_scale, w_out, norm_mlp_g, w_up, w_down, norm_final_g):
    batch, seq, _ = x_prompt.shape
    dec_batch, dec_seq, _ = x_sample.shape
    n_p = batch * seq
    n_s = dec_batch * dec_seq
    xp = x_prompt.reshape(n_p, D_MODEL)
    xs = x_sample.reshape(n_s, D_MODEL)
    row = lambda v: v.reshape(1, -1).astype(_F32)

    wup_pad = jnp.pad(w_alpha_up, ((0, LANES - GATE_RANK), (0, 0))).astype(_BF16)
    wup3 = jnp.concatenate([wup_pad] * 3, axis=0)
    pool_w_b, w_out_b = pool_w.astype(_BF16), w_out.astype(_BF16)
    w_up_b, w_down_b = w_up.astype(_BF16), w_down.astype(_BF16)
    wt = w_in.T

    h, la, u = _norm_gate_u(xp, xs, row(norm_mix_g), wt, wup3, row(b_alpha), tm=512)
    z = _proj(h, wt, 0, QKVR_W // PROJ_TN, _BF16, tm=1024)
    zg = _proj(h, wt, GA_OFF, 2 * D_MODEL // PROJ_TN, _BF16, tm=1024)

    oa_p, s_gla_p = _gla_prompt(z, la, row(gla_norm_g), batch, seq, c=64, rows_per_step=512)
    oa_s, s_gla_s = _gla_sample(z, la, state_gla, row(gla_norm_g), n_p, dec_batch, dec_seq, bb=2)
    buf_p = jnp.stack([u[(b + 1) * seq - POOL_BUF:(b + 1) * seq] for b in range(batch)], axis=0)

    merge_w = (pool_w_b, row(pool_scale), w_out_b, row(norm_mlp_g))
    x1_p, h2_p = _merge_prompt(xp, oa_p, u, zg, *merge_w, batch, seq, tt=512)
    x1_s, h2_s, buf_s = _merge_sample(xs, oa_s, u, state_pool, zg, n_p, *merge_w, dec_seq, bb=32)
    mlp_w = (w_up_b, w_down_b, row(norm_final_g))
    y_p = _mlp(h2_p, x1_p, *mlp_w, tm=1024, tf=512).reshape(batch, seq, D_MODEL)
    y_s = _mlp(h2_s, x1_s, *mlp_w, tm=1024, tf=512).reshape(dec_batch, dec_seq, D_MODEL)
    return (y_p, y_s, s_gla_p, buf_p, s_gla_s, buf_s)
```

```python
import functools

import jax
import jax.numpy as jnp
import numpy as np
from jax import lax
from jax.experimental import pallas as pl
from jax.experimental.pallas import tpu as pltpu

D_MODEL = 2048
GLA_HEADS = 4
GLA_DK = 256
GLA_DV = 512
GATE_RANK = 16
GATE_TAU = 16.0
POOL_WIDTH = 1024
POOL_GROUPS = 4
POOL_GROUP_W = 256
POOL_OUT_GROUP_W = 512
POOL_WINDOWS = (2, 4, 8, 16)
POOL_BUF = 15
D_FF = 8192
EPS = 1e-6
PAST_LEN = 16384

Q_W = GLA_HEADS * GLA_DK
V_W = GLA_HEADS * GLA_DV
QKVR_W = 2 * Q_W + 2 * V_W
A_OFF = QKVR_W
U_OFF = A_OFF + GATE_RANK
GA_OFF = U_OFF + POOL_WIDTH
Z_Q, Z_K, Z_V, Z_R = 0, Q_W, 2 * Q_W, 2 * Q_W + V_W

LANES = 128
PROJ_TN = 1024
HALO = 16
VMEM_LIMIT = 56 * 1024 * 1024

_F32 = jnp.float32
_BF16 = jnp.bfloat16


def _dot(a, b):
    return jnp.dot(a, b, preferred_element_type=_F32)


def _dot_nt(a, b):
    return lax.dot_general(a, b, (((1,), (1,)), ((), ())), preferred_element_type=_F32)


def _dot_tn(a, b):
    return lax.dot_general(a, b, (((0,), (0,)), ((), ())), preferred_element_type=_F32)


def _split3(x):
    hi = x.astype(_BF16)
    r1 = x - hi.astype(_F32)
    mid = r1.astype(_BF16)
    lo = (r1 - mid.astype(_F32)).astype(_BF16)
    return hi, mid, lo


def _rms_scale(x):
    return lax.rsqrt(jnp.mean(x * x, axis=-1, keepdims=True) + EPS)


def _params(*sem):
    return pltpu.CompilerParams(dimension_semantics=sem, vmem_limit_bytes=VMEM_LIMIT)


def _norm_gate_u_kernel(xp_ref, xs_ref, g_ref, wa_ref, wup_ref, ba_ref, wu_ref,
                        h_ref, la_ref, u_ref, wu_scr, *, prompt_tiles):
    i = pl.program_id(0)
    rc = 256

    @pl.when(i == 0)
    def _():
        for r in range(0, POOL_WIDTH, rc):
            wu_scr[r:r + rc, :] = wu_ref[r:r + rc, :].astype(_BF16)

    def compute(x_ref):
        x = x_ref[...]
        hb = (x * _rms_scale(x) * g_ref[...]).astype(_BF16)
        h_ref[...] = hb
        u_ref[...] = _dot_nt(hb, wu_scr[...])
        a_lr = _dot_nt(hb, wa_ref[...].astype(_BF16))
        a_hi, a_mid, a_lo = _split3(a_lr)
        a3 = jnp.concatenate([a_hi, a_mid, a_lo], axis=1)
        a_logit = _dot(a3, wup_ref[...]) + ba_ref[...]
        ls = jnp.minimum(a_logit, 0.0) - jnp.log1p(jnp.exp(-jnp.abs(a_logit)))
        la_ref[...] = ls * (1.0 / GATE_TAU)

    @pl.when(i < prompt_tiles)
    def _():
        compute(xp_ref)

    @pl.when(i >= prompt_tiles)
    def _():
        compute(xs_ref)


def _norm_gate_u(xp, xs, g, wt, wup3, b_alpha, tm):
    n_p, n_s = xp.shape[0], xs.shape[0]
    pt, st = n_p // tm, n_s // tm
    m = n_p + n_s
    const = lambda i: (0, 0)
    return pl.pallas_call(
        functools.partial(_norm_gate_u_kernel, prompt_tiles=pt),
        grid=(pt + st,),
        in_specs=[
            pl.BlockSpec((tm, D_MODEL), lambda i: (jnp.minimum(i, pt - 1), 0)),
            pl.BlockSpec((tm, D_MODEL), lambda i: (jnp.maximum(i - pt, 0), 0)),
            pl.BlockSpec((1, D_MODEL), const),
            pl.BlockSpec((LANES, D_MODEL), lambda i: (A_OFF // LANES, 0)),
            pl.BlockSpec((3 * LANES, Q_W), const),
            pl.BlockSpec((1, Q_W), const),
            pl.BlockSpec((pl.Element(POOL_WIDTH), pl.Element(D_MODEL)), lambda i: (U_OFF, 0),
                         pipeline_mode=pl.Buffered(1)),
        ],
        out_specs=[
            pl.BlockSpec((tm, D_MODEL), lambda i: (i, 0)),
            pl.BlockSpec((tm, Q_W), lambda i: (i, 0)),
            pl.BlockSpec((tm, POOL_WIDTH), lambda i: (i, 0)),
        ],
        out_shape=[
            jax.ShapeDtypeStruct((m, D_MODEL), _BF16),
            jax.ShapeDtypeStruct((m, Q_W), _F32),
            jax.ShapeDtypeStruct((m, POOL_WIDTH), _F32),
        ],
        scratch_shapes=[pltpu.VMEM((POOL_WIDTH, D_MODEL), _BF16)],
        compiler_params=_params("arbitrary"),
        name="norm_gate_u",
    )(xp, xs, g, wt, wup3, b_alpha, wt)


def _proj_kernel(h_ref, wt_ref, z_ref, w_scr):
    rc = 256

    @pl.when(pl.program_id(1) == 0)
    def _():
        for r in range(0, w_scr.shape[0], rc):
            w_scr[r:r + rc, :] = wt_ref[r:r + rc, :].astype(_BF16)

    z_ref[...] = _dot_nt(h_ref[...], w_scr[...]).astype(z_ref.dtype)


def _proj(h, wt, col0, n_tiles, out_dtype, tm, tn=PROJ_TN):
    m = h.shape[0]
    assert col0 % 8 == 0 and col0 + n_tiles * tn <= wt.shape[0]
    return pl.pallas_call(
        _proj_kernel,
        grid=(n_tiles, m // tm),
        in_specs=[
            pl.BlockSpec((tm, D_MODEL), lambda j, i: (i, 0)),
            pl.BlockSpec((pl.Element(tn), pl.Element(D_MODEL)), lambda j, i: (pl.multiple_of(col0 + j * tn, 8), 0)),
        ],
        out_specs=pl.BlockSpec((tm, tn), lambda j, i: (i, j)),
        out_shape=jax.ShapeDtypeStruct((m, n_tiles * tn), out_dtype),
        scratch_shapes=[pltpu.VMEM((tn, D_MODEL), _BF16)],
        compiler_params=_params("arbitrary", "arbitrary"),
        name=f"proj_c{col0}",
    )(h, wt)


def _gla_consts(c, per=1):
    g = c // per
    levels = int(np.log2(g))
    assert 2 ** levels == g and g * per == c
    t = np.arange(c)
    seq_id = t // g
    mats = [np.tril(np.ones((c, c), np.float32)) * (seq_id[:, None] == seq_id[None, :])]
    masks = [np.eye(c, dtype=np.float32)]
    for l in range(levels):
        h = 2 ** l
        blk = t // (2 * h)
        mid = blk * 2 * h + h - 1
        upper = (t // h) % 2 == 1
        m = np.zeros((c, c), np.float32)
        for row in range(c):
            if upper[row]:
                m[row, mid[row] + 1: row + 1] = 1.0
            else:
                m[row, row + 1: mid[row] + 1] = 1.0
        mats.append(m)
        masks.append(((blk[:, None] == blk[None, :]) & upper[:, None] & (~upper)[None, :])
                     .astype(np.float32))
    m_all = np.concatenate(mats, axis=0)
    m_cat = np.concatenate([m_all, m_all, m_all], axis=1)
    rowmasks = np.stack([(seq_id == e).astype(np.float32)[:, None] for e in range(per)])
    ones_cols = np.stack([np.tile(np.broadcast_to(rowmasks[e], (c, LANES)), (3, 1))
                          for e in range(per)])
    return (jnp.asarray(m_cat, _BF16), jnp.asarray(np.stack(masks), _F32),
            jnp.asarray(ones_cols, _BF16), jnp.asarray(rowmasks, _F32), levels)


def _gla_chunks(qs, ks, vbs, la_all, states, m_cat, masks, ones_cols, rowmasks, c, levels):
    n = len(qs)
    per = len(states[0])
    g = c // per
    sl = lambda a, i: a[:, i * GLA_DK:(i + 1) * GLA_DK]
    hi, mid, lo = _split3(la_all)
    la3 = jnp.concatenate([hi, mid, lo], axis=0)
    eb = _dot(m_cat, la3)
    dcols = [jnp.exp(_dot_tn(la3, ones_cols[e])) for e in range(per)]
    b = eb[0:c]
    scores = [jnp.where(masks[0] > 0, _dot_nt(qs[i].astype(_BF16), ks[i].astype(_BF16)), 0.0)
              for i in range(n)]
    for l in range(levels):
        x = jnp.exp(eb[(l + 1) * c:(l + 2) * c])
        for i in range(n):
            xi = sl(x, i)
            s_l = _dot_nt((qs[i] * xi).astype(_BF16), (ks[i] * xi).astype(_BF16))
            scores[i] = jnp.where(masks[l + 1] > 0, s_l, scores[i])
    eb0 = jnp.exp(b)
    ebl = jnp.exp(jnp.concatenate(
        [b[(e + 1) * g - 1:(e + 1) * g, :] - b[e * g:(e + 1) * g, :] for e in range(per)], axis=0))
    pick = (lambda a, e: a) if per == 1 else (lambda a, e: a * rowmasks[e])
    outs, new_states = [], []
    for i in range(n):
        o = _dot(scores[i].astype(_BF16), vbs[i])
        qd = qs[i] * sl(eb0, i)
        for e in range(per):
            o = o + _dot(pick(qd, e).astype(_BF16), states[i][e].astype(_BF16))
        outs.append(o)
    for i in range(n):
        kd = ks[i] * sl(ebl, i)
        upd = []
        for e in range(per):
            di = dcols[e][i * GLA_DK:(i + 1) * GLA_DK]
            decay = jnp.concatenate([di] * (GLA_DV // LANES), axis=1)
            upd.append(decay * states[i][e] + _dot_tn(pick(kd, e).astype(_BF16), vbs[i]))
        new_states.append(upd)
    return outs, new_states


def _gla_post(o, r, g):
    o_n = o * _rms_scale(o) * g
    return o_n * (r * jax.nn.sigmoid(r))


def _load_gla_consts(mc_ref, mk_ref, oc_ref, rm_ref):
    return (mc_ref[...], [mk_ref[i] for i in range(mk_ref.shape[0])],
            [oc_ref[e] for e in range(oc_ref.shape[0])], [rm_ref[e] for e in range(rm_ref.shape[0])])


def _const_specs(arrays):
    return [pl.BlockSpec(a.shape, lambda *idx, nd=a.ndim: (0,) * nd) for a in arrays]


def _gla_rows(q_ref, k_ref, v_ref, r_ref, la_ref, oa_ref, rows, states, consts, c, levels, g):
    dk = lambda h: slice(h * GLA_DK, (h + 1) * GLA_DK)
    dv = lambda h: slice(h * GLA_DV, (h + 1) * GLA_DV)
    heads = range(GLA_HEADS)
    qs = [q_ref[rows, dk(h)].astype(_F32) * (GLA_DK ** -0.5) for h in heads]
    ks = [k_ref[rows, dk(h)].astype(_F32) for h in heads]
    vbs = [v_ref[rows, dv(h)] for h in heads]
    outs, new_states = _gla_chunks(qs, ks, vbs, la_ref[rows, :], states, *consts, c, levels)
    for h in heads:
        oa_ref[rows, dv(h)] = _gla_post(outs[h], r_ref[rows, dv(h)].astype(_F32), g
                                        ).astype(oa_ref.dtype)
    return new_states


def _gla_kernel(qp_ref, kp_ref, vp_ref, rp_ref, lap_ref, qs_ref, ks_ref, vs_ref, rs_ref, las_ref,
                st_ref, g_ref, mcp_ref, mkp_ref, ocp_ref, rmp_ref, mcs_ref, mks_ref, ocs_ref, rms_ref,
                oap_ref, soutp_ref, oas_ref, souts_ref, s_scr, *, c, levels, n_sub, cs, per, levels_s):
    n = pl.program_id(1)

    @pl.when(n == 0)
    def _():
        s_scr[...] = jnp.zeros_like(s_scr)

    consts_p = _load_gla_consts(mcp_ref, mkp_ref, ocp_ref, rmp_ref)
    consts_s = _load_gla_consts(mcs_ref, mks_ref, ocs_ref, rms_ref)
    g = g_ref[...]
    heads = range(GLA_HEADS)
    for i in range(n_sub):
        new = _gla_rows(qp_ref, kp_ref, vp_ref, rp_ref, lap_ref, oap_ref, slice(i * c, (i + 1) * c),
                        [[s_scr[h]] for h in heads], consts_p, c, levels, g)
        for h in heads:
            s_scr[h] = new[h][0]
        if i == 0:
            new_s = _gla_rows(qs_ref, ks_ref, vs_ref, rs_ref, las_ref, oas_ref, slice(0, cs),
                              [[st_ref[e, h] for e in range(per)] for h in heads],
                              consts_s, cs, levels_s, g)
            for h in heads:
                for e in range(per):
                    souts_ref[e, h] = new_s[h][e]

    @pl.when(n == pl.num_programs(1) - 1)
    def _():
        soutp_ref[0] = s_scr[...]


def _gla(z, la, state, gla_g, batch, seq, dec_seq, c, rows_per_step):
    cs = 16
    per = cs // dec_seq
    dec_batch = state.shape[0]
    n_p = batch * seq
    steps = seq // rows_per_step
    n_sub = rows_per_step // c
    assert cs % dec_seq == 0 and batch * steps * per == dec_batch and n_p % cs == 0
    *consts_p, levels = _gla_consts(c)
    *consts_s, levels_s = _gla_consts(cs, per)
    kern = functools.partial(_gla_kernel, c=c, levels=levels, n_sub=n_sub, cs=cs, per=per,
                             levels_s=levels_s)
    rb = lambda b, n: b * steps + n
    sb = lambda b, n: n_p // cs + rb(b, n)
    return pl.pallas_call(
        kern,
        grid=(batch, steps),
        in_specs=[
            pl.BlockSpec((rows_per_step, Q_W), lambda b, n: (rb(b, n), Z_Q // Q_W)),
            pl.BlockSpec((rows_per_step, Q_W), lambda b, n: (rb(b, n), Z_K // Q_W)),
            pl.BlockSpec((rows_per_step, V_W), lambda b, n: (rb(b, n), Z_V // V_W)),
            pl.BlockSpec((rows_per_step, V_W), lambda b, n: (rb(b, n), Z_R // V_W)),
            pl.BlockSpec((rows_per_step, Q_W), lambda b, n: (rb(b, n), 0)),
            pl.BlockSpec((cs, Q_W), lambda b, n: (sb(b, n), Z_Q // Q_W)),
            pl.BlockSpec((cs, Q_W), lambda b, n: (sb(b, n), Z_K // Q_W)),
            pl.BlockSpec((cs, V_W), lambda b, n: (sb(b, n), Z_V // V_W)),
            pl.BlockSpec((cs, V_W), lambda b, n: (sb(b, n), Z_R // V_W)),
            pl.BlockSpec((cs, Q_W), lambda b, n: (sb(b, n), 0)),
            pl.BlockSpec((per, GLA_HEADS, GLA_DK, GLA_DV), lambda b, n: (rb(b, n), 0, 0, 0)),
            pl.BlockSpec((1, GLA_DV), lambda b, n: (0, 0)),
        ] + _const_specs(consts_p) + _const_specs(consts_s),
        out_specs=[
            pl.BlockSpec((rows_per_step, V_W), lambda b, n: (rb(b, n), 0)),
            pl.BlockSpec((1, GLA_HEADS, GLA_DK, GLA_DV), lambda b, n: (b, 0, 0, 0)),
            pl.BlockSpec((cs, V_W), lambda b, n: (rb(b, n), 0)),
            pl.BlockSpec((per, GLA_HEADS, GLA_DK, GLA_DV), lambda b, n: (rb(b, n), 0, 0, 0)),
        ],
        out_shape=[
            jax.ShapeDtypeStruct((n_p, V_W), _BF16),
            jax.ShapeDtypeStruct((batch, GLA_HEADS, GLA_DK, GLA_DV), _F32),
            jax.ShapeDtypeStruct((dec_batch * dec_seq, V_W), _BF16),
            jax.ShapeDtypeStruct((dec_batch, GLA_HEADS, GLA_DK, GLA_DV), _F32),
        ],
        scratch_shapes=[pltpu.VMEM((GLA_HEADS, GLA_DK, GLA_DV), _F32)],
        compiler_params=_params("arbitrary", "arbitrary"),
        name="gla",
    )(z, z, z, z, la, z, z, z, z, la, state, gla_g, *consts_p, *consts_s)


def _window_sums(ext_ref, t0, tt, lead):
    outs = []
    for gi, w in enumerate(POOL_WINDOWS):
        lanes = slice(gi * POOL_GROUP_W, (gi + 1) * POOL_GROUP_W)
        acc = ext_ref[lead + (pl.ds(t0, tt), lanes)]
        for j in range(1, w):
            acc = acc + ext_ref[lead + (pl.ds(t0 - j, tt), lanes)]
        outs.append(acc)
    return outs


def _pool_prompt_tile(u_ref, ext_scr, n, tt):
    @pl.when(n == 0)
    def _():
        ext_scr[0:HALO, :] = jnp.zeros((HALO, POOL_WIDTH), _F32)

    u = u_ref[...]
    ext_scr[HALO:HALO + tt, :] = u
    sums = _window_sums(ext_scr, HALO, tt, ())
    pos = (n * tt + 1 + lax.broadcasted_iota(jnp.int32, (tt, 1), 0)).astype(_F32)
    ps = []
    for gi, w in enumerate(POOL_WINDOWS):
        lanes = slice(gi * POOL_GROUP_W, (gi + 1) * POOL_GROUP_W)
        ps.append(sums[gi] / jnp.minimum(pos, float(w)) - u[:, lanes])
    ext_scr[0:HALO, :] = u[tt - HALO:tt, :]
    return ps


def _pool_sample_tile(u_ref, buf_ref, nb_ref, ext_scr, bb, seq):
    u = u_ref[...].reshape(bb, seq, POOL_WIDTH)
    ext_scr[:, 0:HALO - POOL_BUF, :] = jnp.zeros((bb, HALO - POOL_BUF, POOL_WIDTH), _F32)
    ext_scr[:, HALO - POOL_BUF:HALO, :] = buf_ref[...]
    ext_scr[:, HALO:HALO + seq, :] = u
    sums = _window_sums(ext_scr, HALO, seq, (slice(None),))
    ps = []
    for gi, w in enumerate(POOL_WINDOWS):
        lanes = slice(gi * POOL_GROUP_W, (gi + 1) * POOL_GROUP_W)
        ps.append((sums[gi] / float(w) - u[:, :, lanes]).reshape(bb * seq, POOL_GROUP_W))
    nb_ref[...] = ext_scr[:, HALO + seq - POOL_BUF:HALO + seq, :]
    return ps


def _merge_core(x_ref, oa_ref, ps, ga_ref, gb_ref, pw_ref, ps_ref, wo_ref, g2_ref, x1_ref, h2_ref):
    ob = jnp.concatenate([_dot(ps[gi].astype(_BF16), pw_ref[gi]) for gi in range(POOL_GROUPS)],
                         axis=1) * ps_ref[...]
    m = (jax.nn.sigmoid(ga_ref[...].astype(_F32)) * oa_ref[...].astype(_F32)
         + jax.nn.sigmoid(gb_ref[...].astype(_F32)) * ob)
    x1 = x_ref[...] + _dot(m.astype(_BF16), wo_ref[...])
    x1_ref[...] = x1
    h2_ref[...] = (x1 * _rms_scale(x1) * g2_ref[...]).astype(_BF16)


def _merge_prompt_kernel(x_ref, oa_ref, u_ref, ga_ref, gb_ref, pw_ref, ps_ref, wo_ref, g2_ref,
                         x1_ref, h2_ref, ext_scr, *, tt):
    ps = _pool_prompt_tile(u_ref, ext_scr, pl.program_id(1), tt)
    _merge_core(x_ref, oa_ref, ps, ga_ref, gb_ref, pw_ref, ps_ref, wo_ref, g2_ref, x1_ref, h2_ref)


def _merge_sample_kernel(x_ref, oa_ref, u_ref, buf_ref, ga_ref, gb_ref, pw_ref, ps_ref, wo_ref,
                         g2_ref, x1_ref, h2_ref, nb_ref, ext_scr, *, bb, seq):
    ps = _pool_sample_tile(u_ref, buf_ref, nb_ref, ext_scr, bb, seq)
    _merge_core(x_ref, oa_ref, ps, ga_ref, gb_ref, pw_ref, ps_ref, wo_ref, g2_ref, x1_ref, h2_ref)


def _weight_specs():
    zeros = lambda nd: (lambda *idx: (0,) * nd)
    once = pl.Buffered(1)
    return [
        pl.BlockSpec((POOL_GROUPS, POOL_GROUP_W, POOL_OUT_GROUP_W), zeros(3), pipeline_mode=once),
        pl.BlockSpec((1, D_MODEL), zeros(2)),
        pl.BlockSpec((D_MODEL, D_MODEL), zeros(2), pipeline_mode=once),
        pl.BlockSpec((1, D_MODEL), zeros(2)),
    ]


def _merge_prompt(x, oa, u, zg, pool_w, pool_scale, w_out, g2, batch, seq, tt):
    steps = seq // tt
    rb = lambda b, n: b * steps + n
    return pl.pallas_call(
        functools.partial(_merge_prompt_kernel, tt=tt),
        grid=(batch, steps),
        in_specs=[
            pl.BlockSpec((tt, D_MODEL), lambda b, n: (rb(b, n), 0)),
            pl.BlockSpec((tt, D_MODEL), lambda b, n: (rb(b, n), 0)),
            pl.BlockSpec((tt, POOL_WIDTH), lambda b, n: (rb(b, n), 0)),
            pl.BlockSpec((tt, D_MODEL), lambda b, n: (rb(b, n), 0)),
            pl.BlockSpec((tt, D_MODEL), lambda b, n: (rb(b, n), 1)),
        ] + _weight_specs(),
        out_specs=[
            pl.BlockSpec((tt, D_MODEL), lambda b, n: (rb(b, n), 0)),
            pl.BlockSpec((tt, D_MODEL), lambda b, n: (rb(b, n), 0)),
        ],
        out_shape=[
            jax.ShapeDtypeStruct((batch * seq, D_MODEL), _F32),
            jax.ShapeDtypeStruct((batch * seq, D_MODEL), _BF16),
        ],
        scratch_shapes=[pltpu.VMEM((HALO + tt, POOL_WIDTH), _F32)],
        compiler_params=_params("arbitrary", "arbitrary"),
        name="merge_prompt",
    )(x, oa, u, zg, zg, pool_w, pool_scale, w_out, g2)


def _merge_sample(x, oa, u, buf, zg, row0, pool_w, pool_scale, w_out, g2, seq, bb):
    batch = buf.shape[0]
    rows = bb * seq
    rb0 = row0 // rows
    assert PAST_LEN >= max(POOL_WINDOWS) and row0 % rows == 0 and seq == 8
    return pl.pallas_call(
        functools.partial(_merge_sample_kernel, bb=bb, seq=seq),
        grid=(batch // bb,),
        in_specs=[
            pl.BlockSpec((rows, D_MODEL), lambda i: (i, 0)),
            pl.BlockSpec((rows, D_MODEL), lambda i: (i, 0)),
            pl.BlockSpec((rows, POOL_WIDTH), lambda i: (rb0 + i, 0)),
            pl.BlockSpec((bb, POOL_BUF, POOL_WIDTH), lambda i: (i, 0, 0)),
            pl.BlockSpec((rows, D_MODEL), lambda i: (rb0 + i, 0)),
            pl.BlockSpec((rows, D_MODEL), lambda i: (rb0 + i, 1)),
        ] + _weight_specs(),
        out_specs=[
            pl.BlockSpec((rows, D_MODEL), lambda i: (i, 0)),
            pl.BlockSpec((rows, D_MODEL), lambda i: (i, 0)),
            pl.BlockSpec((bb, POOL_BUF, POOL_WIDTH), lambda i: (i, 0, 0)),
        ],
        out_shape=[
            jax.ShapeDtypeStruct((batch * seq, D_MODEL), _F32),
            jax.ShapeDtypeStruct((batch * seq, D_MODEL), _BF16),
            jax.ShapeDtypeStruct((batch, POOL_BUF, POOL_WIDTH), _F32),
        ],
        scratch_shapes=[pltpu.VMEM((bb, HALO + seq, POOL_WIDTH), _F32)],
        compiler_params=_params("arbitrary"),
        name="merge_sample",
    )(x, oa, u, buf, zg, zg, pool_w, pool_scale, w_out, g2)


def _mlp_kernel(h2_ref, x1_ref, wu_ref, wd_ref, g_ref, y_ref):
    f = pl.program_id(1)

    @pl.when(f == 0)
    def _():
        y_ref[...] = x1_ref[...]

    a = jnp.maximum(_dot(h2_ref[...], wu_ref[...]), 0.0)
    y_ref[...] += _dot((a * a).astype(_BF16), wd_ref[...])

    @pl.when(f == pl.num_programs(1) - 1)
    def _():
        x2 = y_ref[...]
        y_ref[...] = x2 * _rms_scale(x2) * g_ref[...]


def _mlp(h2, x1, w_up, w_down, g, tm, tf):
    m = h2.shape[0]
    return pl.pallas_call(
        _mlp_kernel,
        grid=(m // tm, D_FF // tf),
        in_specs=[
            pl.BlockSpec((tm, D_MODEL), lambda i, f: (i, 0)),
            pl.BlockSpec((tm, D_MODEL), lambda i, f: (i, 0)),
            pl.BlockSpec((D_MODEL, tf), lambda i, f: (0, f)),
            pl.BlockSpec((tf, D_MODEL), lambda i, f: (f, 0)),
            pl.BlockSpec((1, D_MODEL), lambda i, f: (0, 0)),
        ],
        out_specs=pl.BlockSpec((tm, D_MODEL), lambda i, f: (i, 0)),
        out_shape=jax.ShapeDtypeStruct((m, D_MODEL), _F32),
        compiler_params=_params("arbitrary", "arbitrary"),
        name="mlp",
    )(h2, x1, w_up, w_down, g)


def kernel(x_prompt, x_sample, state_gla, state_pool, norm_mix_g, w_in, w_alpha_up, b_alpha,
           gla_norm_g, pool_w, pool_scale, w_out, norm_mlp_g, w_up, w_down, norm_final_g):
    batch, seq, _ = x_prompt.shape
    dec_batch, dec_seq, _ = x_sample.shape
    n_p = batch * seq
    n_s = dec_batch * dec_seq
    xp = x_prompt.reshape(n_p, D_MODEL)
    xs = x_sample.reshape(n_s, D_MODEL)
    row = lambda v: v.reshape(1, -1).astype(_F32)

    wup_pad = jnp.pad(w_alpha_up, ((0, LANES - GATE_RANK), (0, 0))).astype(_BF16)
    wup3 = jnp.concatenate([wup_pad] * 3, axis=0)
    pool_w_b, w_out_b = pool_w.astype(_BF16), w_out.astype(_BF16)
    w_up_b, w_down_b = w_up.astype(_BF16), w_down.astype(_BF16)
    wt = w_in.T

    h, la, u = _norm_gate_u(xp, xs, row(norm_mix_g), wt, wup3, row(b_alpha), tm=512)
    z = _proj(h, wt, 0, QKVR_W // PROJ_TN, _BF16, tm=1024)
    zg = _proj(h, wt, GA_OFF, 2 * D_MODEL // PROJ_TN, _BF16, tm=1024)

    oa_p, s_gla_p, oa_s, s_gla_s = _gla(z, la, state_gla, row(gla_norm_g), batch, seq, dec_seq,
                                        c=64, rows_per_step=128)
    buf_p = jnp.stack([u[(b + 1) * seq - POOL_BUF:(b + 1) * seq] for b in range(batch)], axis=0)

    merge_w = (pool_w_b, row(pool_scale), w_out_b, row(norm_mlp_g))
    x1_p, h2_p = _merge_prompt(xp, oa_p, u, zg, *merge_w, batch, seq, tt=512)
    x1_s, h2_s, buf_s = _merge_sample(xs, oa_s, u, state_pool, zg, n_p, *merge_w, dec_seq, bb=32)
    mlp_w = (w_up_b, w_down_b, row(norm_final_g))
    y_p = _mlp(h2_p, x1_p, *mlp_w, tm=1024, tf=512).reshape(batch, seq, D_MODEL)
    y_s = _mlp(h2_s, x1_s, *mlp_w, tm=1024, tf=512).reshape(dec_batch, dec_seq, D_MODEL)
    return (y_p, y_s, s_gla_p, buf_p, s_gla_s, buf_s)
```

```python
import functools

import jax
import jax.numpy as jnp
import numpy as np
from jax import lax
from jax.experimental import pallas as pl
from jax.experimental.pallas import tpu as pltpu

D_MODEL = 2048
GLA_HEADS = 4
GLA_DK = 256
GLA_DV = 512
GATE_RANK = 16
GATE_TAU = 16.0
POOL_WIDTH = 1024
POOL_GROUPS = 4
POOL_GROUP_W = 256
POOL_OUT_GROUP_W = 512
POOL_WINDOWS = (2, 4, 8, 16)
POOL_BUF = 15
D_FF = 8192
EPS = 1e-6
PAST_LEN = 16384

Q_W = GLA_HEADS * GLA_DK
V_W = GLA_HEADS * GLA_DV
QKVR_W = 2 * Q_W + 2 * V_W
A_OFF = QKVR_W
U_OFF = A_OFF + GATE_RANK
GA_OFF = U_OFF + POOL_WIDTH
Z_Q, Z_K, Z_V, Z_R = 0, Q_W, 2 * Q_W, 2 * Q_W + V_W

LANES = 128
PROJ_TN = 1024
HALO = 16
VMEM_LIMIT = 56 * 1024 * 1024

_F32 = jnp.float32
_BF16 = jnp.bfloat16


def _dot(a, b):
    return jnp.dot(a, b, preferred_element_type=_F32)


def _dot_nt(a, b):
    return lax.dot_general(a, b, (((1,), (1,)), ((), ())), preferred_element_type=_F32)


def _dot_tn(a, b):
    return lax.dot_general(a, b, (((0,), (0,)), ((), ())), preferred_element_type=_F32)


def _split3(x):
    hi = x.astype(_BF16)
    r1 = x - hi.astype(_F32)
    mid = r1.astype(_BF16)
    lo = (r1 - mid.astype(_F32)).astype(_BF16)
    return hi, mid, lo


def _rms_scale(x):
    return lax.rsqrt(jnp.mean(x * x, axis=-1, keepdims=True) + EPS)


def _params(*sem):
    return pltpu.CompilerParams(dimension_semantics=sem, vmem_limit_bytes=VMEM_LIMIT)


def _norm_gate_u_kernel(xp_ref, xs_ref, g_ref, wa_ref, wup_ref, ba_ref, wu_ref,
                        h_ref, la_ref, u_ref, wu_scr, *, prompt_tiles, sub):
    i = pl.program_id(0)
    rc = 256

    @pl.when(i == 0)
    def _():
        for r in range(0, POOL_WIDTH, rc):
            wu_scr[r:r + rc, :] = wu_ref[r:r + rc, :].astype(_BF16)

    def compute(x_ref):
        wa = wa_ref[...].astype(_BF16)
        for r0 in range(0, x_ref.shape[0], sub):
            rows = slice(r0, r0 + sub)
            x = x_ref[rows, :]
            hb = (x * _rms_scale(x) * g_ref[...]).astype(_BF16)
            h_ref[rows, :] = hb
            u_ref[rows, :] = _dot_nt(hb, wu_scr[...])
            a_lr = _dot_nt(hb, wa)
            a_hi, a_mid, a_lo = _split3(a_lr)
            a3 = jnp.concatenate([a_hi, a_mid, a_lo], axis=1)
            a_logit = _dot(a3, wup_ref[...]) + ba_ref[...]
            ls = jnp.minimum(a_logit, 0.0) - jnp.log1p(jnp.exp(-jnp.abs(a_logit)))
            la_ref[rows, :] = ls * (1.0 / GATE_TAU)

    @pl.when(i < prompt_tiles)
    def _():
        compute(xp_ref)

    @pl.when(i >= prompt_tiles)
    def _():
        compute(xs_ref)


def _norm_gate_u(xp, xs, g, wt, wup3, b_alpha, tm):
    n_p, n_s = xp.shape[0], xs.shape[0]
    pt, st = n_p // tm, n_s // tm
    m = n_p + n_s
    const = lambda i: (0, 0)
    return pl.pallas_call(
        functools.partial(_norm_gate_u_kernel, prompt_tiles=pt, sub=min(tm, 256)),
        grid=(pt + st,),
        in_specs=[
            pl.BlockSpec((tm, D_MODEL), lambda i: (jnp.minimum(i, pt - 1), 0)),
            pl.BlockSpec((tm, D_MODEL), lambda i: (jnp.maximum(i - pt, 0), 0)),
            pl.BlockSpec((1, D_MODEL), const),
            pl.BlockSpec((LANES, D_MODEL), lambda i: (A_OFF // LANES, 0)),
            pl.BlockSpec((3 * LANES, Q_W), const),
            pl.BlockSpec((1, Q_W), const),
            pl.BlockSpec((pl.Element(POOL_WIDTH), pl.Element(D_MODEL)), lambda i: (U_OFF, 0),
                         pipeline_mode=pl.Buffered(1)),
        ],
        out_specs=[
            pl.BlockSpec((tm, D_MODEL), lambda i: (i, 0)),
            pl.BlockSpec((tm, Q_W), lambda i: (i, 0)),
            pl.BlockSpec((tm, POOL_WIDTH), lambda i: (i, 0)),
        ],
        out_shape=[
            jax.ShapeDtypeStruct((m, D_MODEL), _BF16),
            jax.ShapeDtypeStruct((m, Q_W), _F32),
            jax.ShapeDtypeStruct((m, POOL_WIDTH), _F32),
        ],
        scratch_shapes=[pltpu.VMEM((POOL_WIDTH, D_MODEL), _BF16)],
        compiler_params=_params("arbitrary"),
        name="norm_gate_u",
    )(xp, xs, g, wt, wup3, b_alpha, wt)


def _proj_kernel(h_ref, wt_ref, z_ref, w_scr):
    rc = 256

    @pl.when(pl.program_id(1) == 0)
    def _():
        for r in range(0, w_scr.shape[0], rc):
            w_scr[r:r + rc, :] = wt_ref[r:r + rc, :].astype(_BF16)

    z_ref[...] = _dot_nt(h_ref[...], w_scr[...]).astype(z_ref.dtype)


def _proj(h, wt, col0, n_tiles, out_dtype, tm, tn=PROJ_TN):
    m = h.shape[0]
    assert col0 % 8 == 0 and col0 + n_tiles * tn <= wt.shape[0]
    return pl.pallas_call(
        _proj_kernel,
        grid=(n_tiles, m // tm),
        in_specs=[
            pl.BlockSpec((tm, D_MODEL), lambda j, i: (i, 0)),
            pl.BlockSpec((pl.Element(tn), pl.Element(D_MODEL)), lambda j, i: (pl.multiple_of(col0 + j * tn, 8), 0)),
        ],
        out_specs=pl.BlockSpec((tm, tn), lambda j, i: (i, j)),
        out_shape=jax.ShapeDtypeStruct((m, n_tiles * tn), out_dtype),
        scratch_shapes=[pltpu.VMEM((tn, D_MODEL), _BF16)],
        compiler_params=_params("arbitrary", "arbitrary"),
        name=f"proj_c{col0}",
    )(h, wt)


def _gla_consts(c, per=1):
    g = c // per
    levels = int(np.log2(g))
    assert 2 ** levels == g and g * per == c
    t = np.arange(c)
    seq_id = t // g
    mats = [np.tril(np.ones((c, c), np.float32)) * (seq_id[:, None] == seq_id[None, :])]
    masks = [np.eye(c, dtype=np.float32)]
    for l in range(levels):
        h = 2 ** l
        blk = t // (2 * h)
        mid = blk * 2 * h + h - 1
        upper = (t // h) % 2 == 1
        m = np.zeros((c, c), np.float32)
        for row in range(c):
            if upper[row]:
                m[row, mid[row] + 1: row + 1] = 1.0
            else:
                m[row, row + 1: mid[row] + 1] = 1.0
        mats.append(m)
        masks.append(((blk[:, None] == blk[None, :]) & upper[:, None] & (~upper)[None, :])
                     .astype(np.float32))
    m_all = np.concatenate(mats, axis=0)
    m_cat = np.concatenate([m_all, m_all, m_all], axis=1)
    rowmasks = np.stack([(seq_id == e).astype(np.float32)[:, None] for e in range(per)])
    ones_cols = np.stack([np.tile(np.broadcast_to(rowmasks[e], (c, LANES)), (3, 1))
                          for e in range(per)])
    return (jnp.asarray(m_cat, _BF16), jnp.asarray(np.stack(masks), _F32),
            jnp.asarray(ones_cols, _BF16), jnp.asarray(rowmasks, _F32), levels)


def _gla_chunks(qs, ks, vbs, la_all, states, m_cat, masks, ones_cols, rowmasks, c, levels):
    n = len(qs)
    per = len(states[0])
    g = c // per
    sl = lambda a, i: a[:, i * GLA_DK:(i + 1) * GLA_DK]
    hi, mid, lo = _split3(la_all)
    la3 = jnp.concatenate([hi, mid, lo], axis=0)
    eb = _dot(m_cat, la3)
    dcols = [jnp.exp(_dot_tn(la3, ones_cols[e])) for e in range(per)]
    b = eb[0:c]
    scores = [jnp.where(masks[0] > 0, _dot_nt(qs[i].astype(_BF16), ks[i].astype(_BF16)), 0.0)
              for i in range(n)]
    for l in range(levels):
        x = jnp.exp(eb[(l + 1) * c:(l + 2) * c])
        for i in range(n):
            xi = sl(x, i)
            s_l = _dot_nt((qs[i] * xi).astype(_BF16), (ks[i] * xi).astype(_BF16))
            scores[i] = jnp.where(masks[l + 1] > 0, s_l, scores[i])
    eb0 = jnp.exp(b)
    ebl = jnp.exp(jnp.concatenate(
        [b[(e + 1) * g - 1:(e + 1) * g, :] - b[e * g:(e + 1) * g, :] for e in range(per)], axis=0))
    pick = (lambda a, e: a) if per == 1 else (lambda a, e: a * rowmasks[e])
    outs, new_states = [], []
    for i in range(n):
        o = _dot(scores[i].astype(_BF16), vbs[i])
        qd = qs[i] * sl(eb0, i)
        for e in range(per):
            o = o + _dot(pick(qd, e).astype(_BF16), states[i][e].astype(_BF16))
        outs.append(o)
    for i in range(n):
        kd = ks[i] * sl(ebl, i)
        upd = []
        for e in range(per):
            di = dcols[e][i * GLA_DK:(i + 1) * GLA_DK]
            decay = jnp.concatenate([di] * (GLA_DV // LANES), axis=1)
            upd.append(decay * states[i][e] + _dot_tn(pick(kd, e).astype(_BF16), vbs[i]))
        new_states.append(upd)
    return outs, new_states


def _gla_post(o, r, g):
    o_n = o * _rms_scale(o) * g
    return o_n * (r * jax.nn.sigmoid(r))


def _load_gla_consts(mc_ref, mk_ref, oc_ref, rm_ref):
    return (mc_ref[...], [mk_ref[i] for i in range(mk_ref.shape[0])],
            [oc_ref[e] for e in range(oc_ref.shape[0])], [rm_ref[e] for e in range(rm_ref.shape[0])])


def _const_specs(arrays):
    return [pl.BlockSpec(a.shape, lambda *idx, nd=a.ndim: (0,) * nd) for a in arrays]


def _gla_rows(q_ref, k_ref, v_ref, r_ref, la_ref, oa_ref, rows, states, consts, c, levels, g):
    dk = lambda h: slice(h * GLA_DK, (h + 1) * GLA_DK)
    dv = lambda h: slice(h * GLA_DV, (h + 1) * GLA_DV)
    heads = range(GLA_HEADS)
    qs = [q_ref[rows, dk(h)].astype(_F32) * (GLA_DK ** -0.5) for h in heads]
    ks = [k_ref[rows, dk(h)].astype(_F32) for h in heads]
    vbs = [v_ref[rows, dv(h)] for h in heads]
    outs, new_states = _gla_chunks(qs, ks, vbs, la_ref[rows, :], states, *consts, c, levels)
    for h in heads:
        oa_ref[rows, dv(h)] = _gla_post(outs[h], r_ref[rows, dv(h)].astype(_F32), g
                                        ).astype(oa_ref.dtype)
    return new_states


def _gla_kernel(qp_ref, kp_ref, vp_ref, rp_ref, lap_ref, qs_ref, ks_ref, vs_ref, rs_ref, las_ref,
                st_ref, g_ref, mcp_ref, mkp_ref, ocp_ref, rmp_ref, mcs_ref, mks_ref, ocs_ref, rms_ref,
                oap_ref, soutp_ref, oas_ref, souts_ref, s_scr, *, c, levels, n_sub, cs, per, levels_s):
    n = pl.program_id(1)

    @pl.when(n == 0)
    def _():
        s_scr[...] = jnp.zeros_like(s_scr)

    consts_p = _load_gla_consts(mcp_ref, mkp_ref, ocp_ref, rmp_ref)
    consts_s = _load_gla_consts(mcs_ref, mks_ref, ocs_ref, rms_ref)
    g = g_ref[...]
    heads = range(GLA_HEADS)
    for i in range(n_sub):
        new = _gla_rows(qp_ref, kp_ref, vp_ref, rp_ref, lap_ref, oap_ref, slice(i * c, (i + 1) * c),
                        [[s_scr[h]] for h in heads], consts_p, c, levels, g)
        for h in heads:
            s_scr[h] = new[h][0]
        if i == 0:
            new_s = _gla_rows(qs_ref, ks_ref, vs_ref, rs_ref, las_ref, oas_ref, slice(0, cs),
                              [[st_ref[e, h] for e in range(per)] for h in heads],
                              consts_s, cs, levels_s, g)
            for h in heads:
                for e in range(per):
                    souts_ref[e, h] = new_s[h][e]

    @pl.when(n == pl.num_programs(1) - 1)
    def _():
        soutp_ref[0] = s_scr[...]


def _gla(z, la, state, gla_g, batch, seq, dec_seq, c, rows_per_step):
    cs = 16
    per = cs // dec_seq
    dec_batch = state.shape[0]
    n_p = batch * seq
    steps = seq // rows_per_step
    n_sub = rows_per_step // c
    assert cs % dec_seq == 0 and batch * steps * per == dec_batch and n_p % cs == 0
    *consts_p, levels = _gla_consts(c)
    *consts_s, levels_s = _gla_consts(cs, per)
    kern = functools.partial(_gla_kernel, c=c, levels=levels, n_sub=n_sub, cs=cs, per=per,
                             levels_s=levels_s)
    rb = lambda b, n: b * steps + n
    sb = lambda b, n: n_p // cs + rb(b, n)
    return pl.pallas_call(
        kern,
        grid=(batch, steps),
        in_specs=[
            pl.BlockSpec((rows_per_step, Q_W), lambda b, n: (rb(b, n), Z_Q // Q_W)),
            pl.BlockSpec((rows_per_step, Q_W), lambda b, n: (rb(b, n), Z_K // Q_W)),
            pl.BlockSpec((rows_per_step, V_W), lambda b, n: (rb(b, n), Z_V // V_W)),
            pl.BlockSpec((rows_per_step, V_W), lambda b, n: (rb(b, n), Z_R // V_W)),
            pl.BlockSpec((rows_per_step, Q_W), lambda b, n: (rb(b, n), 0)),
            pl.BlockSpec((cs, Q_W), lambda b, n: (sb(b, n), Z_Q // Q_W)),
            pl.BlockSpec((cs, Q_W), lambda b, n: (sb(b, n), Z_K // Q_W)),
            pl.BlockSpec((cs, V_W), lambda b, n: (sb(b, n), Z_V // V_W)),
            pl.BlockSpec((cs, V_W), lambda b, n: (sb(b, n), Z_R // V_W)),
            pl.BlockSpec((cs, Q_W), lambda b, n: (sb(b, n), 0)),
            pl.BlockSpec((per, GLA_HEADS, GLA_DK, GLA_DV), lambda b, n: (rb(b, n), 0, 0, 0)),
            pl.BlockSpec((1, GLA_DV), lambda b, n: (0, 0)),
        ] + _const_specs(consts_p) + _const_specs(consts_s),
        out_specs=[
            pl.BlockSpec((rows_per_step, V_W), lambda b, n: (rb(b, n), 0)),
            pl.BlockSpec((1, GLA_HEADS, GLA_DK, GLA_DV), lambda b, n: (b, 0, 0, 0)),
            pl.BlockSpec((cs, V_W), lambda b, n: (rb(b, n), 0)),
            pl.BlockSpec((per, GLA_HEADS, GLA_DK, GLA_DV), lambda b, n: (rb(b, n), 0, 0, 0)),
        ],
        out_shape=[
            jax.ShapeDtypeStruct((n_p, V_W), _BF16),
            jax.ShapeDtypeStruct((batch, GLA_HEADS, GLA_DK, GLA_DV), _F32),
            jax.ShapeDtypeStruct((dec_batch * dec_seq, V_W), _BF16),
            jax.ShapeDtypeStruct((dec_batch, GLA_HEADS, GLA_DK, GLA_DV), _F32),
        ],
        scratch_shapes=[pltpu.VMEM((GLA_HEADS, GLA_DK, GLA_DV), _F32)],
        compiler_params=_params("arbitrary", "arbitrary"),
        name="gla",
    )(z, z, z, z, la, z, z, z, z, la, state, gla_g, *consts_p, *consts_s)


def _window_sums(ext_ref, t0, tt, lead):
    axis = len(lead)
    outs = []
    for gi, w in enumerate(POOL_WINDOWS):
        assert w & (w - 1) == 0 and w - 1 <= HALO
        lanes = slice(gi * POOL_GROUP_W, (gi + 1) * POOL_GROUP_W)
        s = ext_ref[lead + (slice(t0 - HALO, t0 + tt), lanes)]
        span = 1
        while span < w:
            s = s + pltpu.roll(s, span, axis=axis)
            span *= 2
        outs.append(s[lead + (slice(HALO, HALO + tt),)])
    return outs


def _pool_prompt_rows(ext_scr, pos0, r0, nr):
    sums = _window_sums(ext_scr, HALO + r0, nr, ())
    pos = (pos0 + r0 + 1 + lax.broadcasted_iota(jnp.int32, (nr, 1), 0)).astype(_F32)
    ps = []
    for gi, w in enumerate(POOL_WINDOWS):
        lanes = slice(gi * POOL_GROUP_W, (gi + 1) * POOL_GROUP_W)
        ps.append(sums[gi] / jnp.minimum(pos, float(w)) - ext_scr[HALO + r0:HALO + r0 + nr, lanes])
    return ps


def _pool_sample_tile(u_ref, buf_ref, nb_ref, ext_scr, bb, seq):
    u = u_ref[...].reshape(bb, seq, POOL_WIDTH)
    ext_scr[:, 0:HALO - POOL_BUF, :] = jnp.zeros((bb, HALO - POOL_BUF, POOL_WIDTH), _F32)
    ext_scr[:, HALO - POOL_BUF:HALO, :] = buf_ref[...]
    ext_scr[:, HALO:HALO + seq, :] = u
    sums = _window_sums(ext_scr, HALO, seq, (slice(None),))
    ps = []
    for gi, w in enumerate(POOL_WINDOWS):
        lanes = slice(gi * POOL_GROUP_W, (gi + 1) * POOL_GROUP_W)
        ps.append((sums[gi] / float(w) - u[:, :, lanes]).reshape(bb * seq, POOL_GROUP_W))
    nb_ref[...] = ext_scr[:, HALO + seq - POOL_BUF:HALO + seq, :]
    return ps


def _merge_core(rows, x_ref, oa_ref, ps, ga_ref, gb_ref, pw_ref, ps_ref, wo_ref, g2_ref,
                x1_ref, h2_ref):
    ob = jnp.concatenate([_dot(ps[gi].astype(_BF16), pw_ref[gi]) for gi in range(POOL_GROUPS)],
                         axis=1) * ps_ref[...]
    m = (jax.nn.sigmoid(ga_ref[rows, :].astype(_F32)) * oa_ref[rows, :].astype(_F32)
         + jax.nn.sigmoid(gb_ref[rows, :].astype(_F32)) * ob)
    x1 = x_ref[rows, :] + _dot(m.astype(_BF16), wo_ref[...])
    x1_ref[rows, :] = x1
    h2_ref[rows, :] = (x1 * _rms_scale(x1) * g2_ref[...]).astype(_BF16)


def _merge_prompt_kernel(x_ref, oa_ref, u_ref, ga_ref, gb_ref, pw_ref, ps_ref, wo_ref, g2_ref,
                         x1_ref, h2_ref, ext_scr, *, tt, sub):
    n = pl.program_id(1)

    @pl.when(n == 0)
    def _():
        ext_scr[0:HALO, :] = jnp.zeros((HALO, POOL_WIDTH), _F32)

    ext_scr[HALO:HALO + tt, :] = u_ref[...]
    for r0 in range(0, tt, sub):
        ps = _pool_prompt_rows(ext_scr, n * tt, r0, sub)
        _merge_core(slice(r0, r0 + sub), x_ref, oa_ref, ps, ga_ref, gb_ref, pw_ref, ps_ref, wo_ref,
                    g2_ref, x1_ref, h2_ref)
    ext_scr[0:HALO, :] = ext_scr[tt:tt + HALO, :]


def _merge_sample_kernel(x_ref, oa_ref, u_ref, buf_ref, ga_ref, gb_ref, pw_ref, ps_ref, wo_ref,
                         g2_ref, x1_ref, h2_ref, nb_ref, ext_scr, *, bb, seq):
    ps = _pool_sample_tile(u_ref, buf_ref, nb_ref, ext_scr, bb, seq)
    _merge_core(slice(None), x_ref, oa_ref, ps, ga_ref, gb_ref, pw_ref, ps_ref, wo_ref, g2_ref,
                x1_ref, h2_ref)


def _weight_specs():
    zeros = lambda nd: (lambda *idx: (0,) * nd)
    once = pl.Buffered(1)
    return [
        pl.BlockSpec((POOL_GROUPS, POOL_GROUP_W, POOL_OUT_GROUP_W), zeros(3), pipeline_mode=once),
        pl.BlockSpec((1, D_MODEL), zeros(2)),
        pl.BlockSpec((D_MODEL, D_MODEL), zeros(2), pipeline_mode=once),
        pl.BlockSpec((1, D_MODEL), zeros(2)),
    ]


def _merge_prompt(x, oa, u, zg, pool_w, pool_scale, w_out, g2, batch, seq, tt):
    steps = seq // tt
    rb = lambda b, n: b * steps + n
    return pl.pallas_call(
        functools.partial(_merge_prompt_kernel, tt=tt, sub=256),
        grid=(batch, steps),
        in_specs=[
            pl.BlockSpec((tt, D_MODEL), lambda b, n: (rb(b, n), 0)),
            pl.BlockSpec((tt, D_MODEL), lambda b, n: (rb(b, n), 0)),
            pl.BlockSpec((tt, POOL_WIDTH), lambda b, n: (rb(b, n), 0)),
            pl.BlockSpec((tt, D_MODEL), lambda b, n: (rb(b, n), 0)),
            pl.BlockSpec((tt, D_MODEL), lambda b, n: (rb(b, n), 1)),
        ] + _weight_specs(),
        out_specs=[
            pl.BlockSpec((tt, D_MODEL), lambda b, n: (rb(b, n), 0)),
            pl.BlockSpec((tt, D_MODEL), lambda b, n: (rb(b, n), 0)),
        ],
        out_shape=[
            jax.ShapeDtypeStruct((batch * seq, D_MODEL), _F32),
            jax.ShapeDtypeStruct((batch * seq, D_MODEL), _BF16),
        ],
        scratch_shapes=[pltpu.VMEM((HALO + tt, POOL_WIDTH), _F32)],
        compiler_params=_params("arbitrary", "arbitrary"),
        name="merge_prompt",
    )(x, oa, u, zg, zg, pool_w, pool_scale, w_out, g2)


def _merge_sample(x, oa, u, buf, zg, row0, pool_w, pool_scale, w_out, g2, seq, bb):
    batch = buf.shape[0]
    rows = bb * seq
    rb0 = row0 // rows
    assert PAST_LEN >= max(POOL_WINDOWS) and row0 % rows == 0 and seq == 8
    return pl.pallas_call(
        functools.partial(_merge_sample_kernel, bb=bb, seq=seq),
        grid=(batch // bb,),
        in_specs=[
            pl.BlockSpec((rows, D_MODEL), lambda i: (i, 0)),
            pl.BlockSpec((rows, D_MODEL), lambda i: (i, 0)),
            pl.BlockSpec((rows, POOL_WIDTH), lambda i: (rb0 + i, 0)),
            pl.BlockSpec((bb, POOL_BUF, POOL_WIDTH), lambda i: (i, 0, 0)),
            pl.BlockSpec((rows, D_MODEL), lambda i: (rb0 + i, 0)),
            pl.BlockSpec((rows, D_MODEL), lambda i: (rb0 + i, 1)),
        ] + _weight_specs(),
        out_specs=[
            pl.BlockSpec((rows, D_MODEL), lambda i: (i, 0)),
            pl.BlockSpec((rows, D_MODEL), lambda i: (i, 0)),
            pl.BlockSpec((bb, POOL_BUF, POOL_WIDTH), lambda i: (i, 0, 0)),
        ],
        out_shape=[
            jax.ShapeDtypeStruct((batch * seq, D_MODEL), _F32),
            jax.ShapeDtypeStruct((batch * seq, D_MODEL), _BF16),
            jax.ShapeDtypeStruct((batch, POOL_BUF, POOL_WIDTH), _F32),
        ],
        scratch_shapes=[pltpu.VMEM((bb, HALO + seq, POOL_WIDTH), _F32)],
        compiler_params=_params("arbitrary"),
        name="merge_sample",
    )(x, oa, u, buf, zg, zg, pool_w, pool_scale, w_out, g2)


def _mlp_kernel(h2_ref, x1_hbm, wu_ref, wd_ref, g_ref, y_ref, x1_scr, x1_sem):
    i = pl.program_id(0)
    f = pl.program_id(1)
    tm = y_ref.shape[0]

    def x1_copy():
        return pltpu.make_async_copy(x1_hbm.at[pl.ds(pl.multiple_of(i * tm, tm), tm), :],
                                     x1_scr, x1_sem)

    @pl.when(f == 0)
    def _():
        x1_copy().start()
        y_ref[...] = jnp.zeros_like(y_ref)

    a = jnp.maximum(_dot(h2_ref[...], wu_ref[...].astype(_BF16)), 0.0)
    y_ref[...] += _dot((a * a).astype(_BF16), wd_ref[...].astype(_BF16))

    @pl.when(f == pl.num_programs(1) - 1)
    def _():
        x1_copy().wait()
        x2 = y_ref[...] + x1_scr[...]
        y_ref[...] = x2 * _rms_scale(x2) * g_ref[...]


def _mlp(h2, x1, w_up, w_down, g, tm, tf):
    m = h2.shape[0]
    assert m % tm == 0
    return pl.pallas_call(
        _mlp_kernel,
        grid=(m // tm, D_FF // tf),
        in_specs=[
            pl.BlockSpec((tm, D_MODEL), lambda i, f: (i, 0), pipeline_mode=pl.Buffered(1)),
            pl.BlockSpec(memory_space=pl.ANY),
            pl.BlockSpec((D_MODEL, tf), lambda i, f: (0, f)),
            pl.BlockSpec((tf, D_MODEL), lambda i, f: (f, 0)),
            pl.BlockSpec((1, D_MODEL), lambda i, f: (0, 0)),
        ],
        out_specs=pl.BlockSpec((tm, D_MODEL), lambda i, f: (i, 0)),
        out_shape=jax.ShapeDtypeStruct((m, D_MODEL), _F32),
        scratch_shapes=[pltpu.VMEM((tm, D_MODEL), _F32), pltpu.SemaphoreType.DMA(())],
        compiler_params=_params("arbitrary", "arbitrary"),
        name="mlp",
    )(h2, x1, w_up, w_down, g)


def kernel(x_prompt, x_sample, state_gla, state_pool, norm_mix_g, w_in, w_alpha_up, b_alpha,
           gla_norm_g, pool_w, pool_scale, w_out, norm_mlp_g, w_up, w_down, norm_final_g):
    batch, seq, _ = x_prompt.shape
    dec_batch, dec_seq, _ = x_sample.shape
    n_p = batch * seq
    n_s = dec_batch * dec_seq
    xp = x_prompt.reshape(n_p, D_MODEL)
    xs = x_sample.reshape(n_s, D_MODEL)
    row = lambda v: v.reshape(1, -1).astype(_F32)

    wup_pad = jnp.pad(w_alpha_up, ((0, LANES - GATE_RANK), (0, 0))).astype(_BF16)
    wup3 = jnp.concatenate([wup_pad] * 3, axis=0)
    pool_w_b, w_out_b = pool_w.astype(_BF16), w_out.astype(_BF16)
    wt = w_in.T

    h, la, u = _norm_gate_u(xp, xs, row(norm_mix_g), wt, wup3, row(b_alpha), tm=512)
    z = _proj(h, wt, 0, QKVR_W // PROJ_TN, _BF16, tm=1024)
    zg = _proj(h, wt, GA_OFF, 2 * D_MODEL // PROJ_TN, _BF16, tm=1024)

    oa_p, s_gla_p, oa_s, s_gla_s = _gla(z, la, state_gla, row(gla_norm_g), batch, seq, dec_seq,
                                        c=64, rows_per_step=128)
    buf_p = jnp.stack([u[(b + 1) * seq - POOL_BUF:(b + 1) * seq] for b in range(batch)], axis=0)

    merge_w = (pool_w_b, row(pool_scale), w_out_b, row(norm_mlp_g))
    x1_p, h2_p = _merge_prompt(xp, oa_p, u, zg, *merge_w, batch, seq, tt=512)
    x1_s, h2_s, buf_s = _merge_sample(xs, oa_s, u, state_pool, zg, n_p, *merge_w, dec_seq, bb=32)
    mlp_w = (w_up, w_down, row(norm_final_g))
    y_p = _mlp(h2_p, x1_p, *mlp_w, tm=1024, tf=512).reshape(batch, seq, D_MODEL)
    y_s = _mlp(h2_s, x1_s, *mlp_w, tm=1024, tf=512).reshape(dec_batch, dec_seq, D_MODEL)
    return (y_p, y_s, s_gla_p, buf_p, s_gla_s, buf_s)
```

```python
import functools

import jax
import jax.numpy as jnp
import numpy as np
from jax import lax
from jax.experimental import pallas as pl
from jax.experimental.pallas import tpu as pltpu

D_MODEL = 2048
GLA_HEADS = 4
GLA_DK = 256
GLA_DV = 512
GATE_RANK = 16
GATE_TAU = 16.0
POOL_WIDTH = 1024
POOL_GROUPS = 4
POOL_GROUP_W = 256
POOL_OUT_GROUP_W = 512
POOL_WINDOWS = (2, 4, 8, 16)
POOL_BUF = 15
D_FF = 8192
EPS = 1e-6
PAST_LEN = 16384

Q_W = GLA_HEADS * GLA_DK
V_W = GLA_HEADS * GLA_DV
QKVR_W = 2 * Q_W + 2 * V_W
A_OFF = QKVR_W
U_OFF = A_OFF + GATE_RANK
GA_OFF = U_OFF + POOL_WIDTH
Z_Q, Z_K, Z_V, Z_R = 0, Q_W, 2 * Q_W, 2 * Q_W + V_W

LANES = 128
PROJ_TN = 1024
HALO = 16
VMEM_LIMIT = 56 * 1024 * 1024

_F32 = jnp.float32
_BF16 = jnp.bfloat16


def _dot(a, b):
    return jnp.dot(a, b, preferred_element_type=_F32)


def _dot_nt(a, b):
    return lax.dot_general(a, b, (((1,), (1,)), ((), ())), preferred_element_type=_F32)


def _dot_tn(a, b):
    return lax.dot_general(a, b, (((0,), (0,)), ((), ())), preferred_element_type=_F32)


def _split3(x):
    hi = x.astype(_BF16)
    r1 = x - hi.astype(_F32)
    mid = r1.astype(_BF16)
    lo = (r1 - mid.astype(_F32)).astype(_BF16)
    return hi, mid, lo


def _rms_scale(x):
    return lax.rsqrt(jnp.mean(x * x, axis=-1, keepdims=True) + EPS)


def _params(*sem):
    return pltpu.CompilerParams(dimension_semantics=sem, vmem_limit_bytes=VMEM_LIMIT)


def _norm_gate_u_kernel(xp_ref, xs_ref, g_ref, wa_ref, wup_ref, ba_ref, wu_ref,
                        h_ref, la_ref, u_ref, wu_scr, *, prompt_tiles, sub):
    i = pl.program_id(0)
    rc = 256

    @pl.when(i == 0)
    def _():
        for r in range(0, POOL_WIDTH, rc):
            wu_scr[r:r + rc, :] = wu_ref[r:r + rc, :].astype(_BF16)

    def compute(x_ref):
        wa = wa_ref[...].astype(_BF16)
        for r0 in range(0, x_ref.shape[0], sub):
            rows = slice(r0, r0 + sub)
            x = x_ref[rows, :]
            hb = (x * _rms_scale(x) * g_ref[...]).astype(_BF16)
            h_ref[rows, :] = hb
            u_ref[rows, :] = _dot_nt(hb, wu_scr[...])
            a_lr = _dot_nt(hb, wa)
            a_hi, a_mid, a_lo = _split3(a_lr)
            a3 = jnp.concatenate([a_hi, a_mid, a_lo], axis=1)
            a_logit = _dot(a3, wup_ref[...]) + ba_ref[...]
            ls = jnp.minimum(a_logit, 0.0) - jnp.log1p(jnp.exp(-jnp.abs(a_logit)))
            la_ref[rows, :] = ls * (1.0 / GATE_TAU)

    @pl.when(i < prompt_tiles)
    def _():
        compute(xp_ref)

    @pl.when(i >= prompt_tiles)
    def _():
        compute(xs_ref)


def _norm_gate_u(xp, xs, g, wt, wup3, b_alpha, tm):
    n_p, n_s = xp.shape[0], xs.shape[0]
    pt, st = n_p // tm, n_s // tm
    m = n_p + n_s
    const = lambda i: (0, 0)
    return pl.pallas_call(
        functools.partial(_norm_gate_u_kernel, prompt_tiles=pt, sub=min(tm, 256)),
        grid=(pt + st,),
        in_specs=[
            pl.BlockSpec((tm, D_MODEL), lambda i: (jnp.minimum(i, pt - 1), 0)),
            pl.BlockSpec((tm, D_MODEL), lambda i: (jnp.maximum(i - pt, 0), 0)),
            pl.BlockSpec((1, D_MODEL), const),
            pl.BlockSpec((LANES, D_MODEL), lambda i: (A_OFF // LANES, 0)),
            pl.BlockSpec((3 * LANES, Q_W), const),
            pl.BlockSpec((1, Q_W), const),
            pl.BlockSpec((pl.Element(POOL_WIDTH), pl.Element(D_MODEL)), lambda i: (U_OFF, 0),
                         pipeline_mode=pl.Buffered(1)),
        ],
        out_specs=[
            pl.BlockSpec((tm, D_MODEL), lambda i: (i, 0)),
            pl.BlockSpec((tm, Q_W), lambda i: (i, 0)),
            pl.BlockSpec((tm, POOL_WIDTH), lambda i: (i, 0)),
        ],
        out_shape=[
            jax.ShapeDtypeStruct((m, D_MODEL), _BF16),
            jax.ShapeDtypeStruct((m, Q_W), _F32),
            jax.ShapeDtypeStruct((m, POOL_WIDTH), _F32),
        ],
        scratch_shapes=[pltpu.VMEM((POOL_WIDTH, D_MODEL), _BF16)],
        compiler_params=_params("arbitrary"),
        name="norm_gate_u",
    )(xp, xs, g, wt, wup3, b_alpha, wt)


def _proj_kernel(h_ref, wt_ref, side_ref, z_ref, side_out_ref, w_scr, *, n_side):
    rc = 256

    @pl.when(pl.program_id(1) == 0)
    def _():
        for r in range(0, w_scr.shape[0], rc):
            w_scr[r:r + rc, :] = wt_ref[r:r + rc, :].astype(_BF16)

    z_ref[...] = _dot_nt(h_ref[...], w_scr[...]).astype(z_ref.dtype)

    step = pl.program_id(0) * pl.num_programs(1) + pl.program_id(1)

    @pl.when(step < n_side)
    def _():
        side_out_ref[...] = side_ref[...].astype(_BF16)


def _proj(h, wt, col0, n_tiles, out_dtype, tm, side, side_block, tn=PROJ_TN):
    m = h.shape[0]
    assert col0 % 8 == 0 and col0 + n_tiles * tn <= wt.shape[0]
    n_i = m // tm
    sr, sc = side.shape[0] // side_block[0], side.shape[1] // side_block[1]
    n_side = sr * sc
    assert n_side <= n_tiles * n_i and sr * side_block[0] == side.shape[0]

    def side_idx(j, i):
        blk = jnp.minimum(j * n_i + i, n_side - 1)
        return (blk // sc, blk % sc)

    return pl.pallas_call(
        functools.partial(_proj_kernel, n_side=n_side),
        grid=(n_tiles, n_i),
        in_specs=[
            pl.BlockSpec((tm, D_MODEL), lambda j, i: (i, 0)),
            pl.BlockSpec((pl.Element(tn), pl.Element(D_MODEL)),
                         lambda j, i: (pl.multiple_of(col0 + j * tn, 8), 0)),
            pl.BlockSpec(side_block, side_idx),
        ],
        out_specs=[
            pl.BlockSpec((tm, tn), lambda j, i: (i, j)),
            pl.BlockSpec(side_block, side_idx),
        ],
        out_shape=[
            jax.ShapeDtypeStruct((m, n_tiles * tn), out_dtype),
            jax.ShapeDtypeStruct(side.shape, _BF16),
        ],
        scratch_shapes=[pltpu.VMEM((tn, D_MODEL), _BF16)],
        compiler_params=_params("arbitrary", "arbitrary"),
        name=f"proj_c{col0}",
    )(h, wt, side)


def _gla_consts(c, per=1):
    g = c // per
    levels = int(np.log2(g))
    assert 2 ** levels == g and g * per == c
    t = np.arange(c)
    seq_id = t // g
    mats = [np.tril(np.ones((c, c), np.float32)) * (seq_id[:, None] == seq_id[None, :])]
    masks = [np.eye(c, dtype=np.float32)]
    for l in range(levels):
        h = 2 ** l
        blk = t // (2 * h)
        mid = blk * 2 * h + h - 1
        upper = (t // h) % 2 == 1
        m = np.zeros((c, c), np.float32)
        for row in range(c):
            if upper[row]:
                m[row, mid[row] + 1: row + 1] = 1.0
            else:
                m[row, row + 1: mid[row] + 1] = 1.0
        mats.append(m)
        masks.append(((blk[:, None] == blk[None, :]) & upper[:, None] & (~upper)[None, :])
                     .astype(np.float32))
    m_all = np.concatenate(mats, axis=0)
    m_cat = np.concatenate([m_all, m_all, m_all], axis=1)
    rowmasks = np.stack([(seq_id == e).astype(np.float32)[:, None] for e in range(per)])
    ones_cols = np.stack([np.tile(np.broadcast_to(rowmasks[e], (c, LANES)), (3, 1))
                          for e in range(per)])
    return (jnp.asarray(m_cat, _BF16), jnp.asarray(np.stack(masks), _F32),
            jnp.asarray(ones_cols, _BF16), jnp.asarray(rowmasks, _F32), levels)


def _gla_chunks(qs, ks, vbs, la_all, states, m_cat, masks, ones_cols, rowmasks, c, levels):
    n = len(qs)
    per = len(states[0])
    g = c // per
    sl = lambda a, i: a[:, i * GLA_DK:(i + 1) * GLA_DK]
    hi, mid, lo = _split3(la_all)
    la3 = jnp.concatenate([hi, mid, lo], axis=0)
    eb = _dot(m_cat, la3)
    dcols = [jnp.exp(_dot_tn(la3, ones_cols[e])) for e in range(per)]
    b = eb[0:c]
    scores = [jnp.where(masks[0] > 0, _dot_nt(qs[i].astype(_BF16), ks[i].astype(_BF16)), 0.0)
              for i in range(n)]
    for l in range(levels):
        x = jnp.exp(eb[(l + 1) * c:(l + 2) * c])
        for i in range(n):
            xi = sl(x, i)
            s_l = _dot_nt((qs[i] * xi).astype(_BF16), (ks[i] * xi).astype(_BF16))
            scores[i] = jnp.where(masks[l + 1] > 0, s_l, scores[i])
    eb0 = jnp.exp(b)
    ebl = jnp.exp(jnp.concatenate(
        [b[(e + 1) * g - 1:(e + 1) * g, :] - b[e * g:(e + 1) * g, :] for e in range(per)], axis=0))
    pick = (lambda a, e: a) if per == 1 else (lambda a, e: a * rowmasks[e])
    outs, new_states = [], []
    for i in range(n):
        o = _dot(scores[i].astype(_BF16), vbs[i])
        qd = qs[i] * sl(eb0, i)
        for e in range(per):
            o = o + _dot(pick(qd, e).astype(_BF16), states[i][e].astype(_BF16))
        outs.append(o)
    for i in range(n):
        kd = ks[i] * sl(ebl, i)
        upd = []
        for e in range(per):
            di = dcols[e][i * GLA_DK:(i + 1) * GLA_DK]
            decay = jnp.concatenate([di] * (GLA_DV // LANES), axis=1)
            upd.append(decay * states[i][e] + _dot_tn(pick(kd, e).astype(_BF16), vbs[i]))
        new_states.append(upd)
    return outs, new_states


def _gla_post(o, r, g):
    o_n = o * _rms_scale(o) * g
    return o_n * (r * jax.nn.sigmoid(r))


def _load_gla_consts(mc_ref, mk_ref, oc_ref, rm_ref):
    return (mc_ref[...], [mk_ref[i] for i in range(mk_ref.shape[0])],
            [oc_ref[e] for e in range(oc_ref.shape[0])], [rm_ref[e] for e in range(rm_ref.shape[0])])


def _const_specs(arrays):
    return [pl.BlockSpec(a.shape, lambda *idx, nd=a.ndim: (0,) * nd) for a in arrays]


def _gla_rows(q_ref, k_ref, v_ref, r_ref, la_ref, oa_ref, rows, states, consts, c, levels, g):
    dk = lambda h: slice(h * GLA_DK, (h + 1) * GLA_DK)
    dv = lambda h: slice(h * GLA_DV, (h + 1) * GLA_DV)
    heads = range(GLA_HEADS)
    qs = [q_ref[rows, dk(h)].astype(_F32) * (GLA_DK ** -0.5) for h in heads]
    ks = [k_ref[rows, dk(h)].astype(_F32) for h in heads]
    vbs = [v_ref[rows, dv(h)] for h in heads]
    outs, new_states = _gla_chunks(qs, ks, vbs, la_ref[rows, :], states, *consts, c, levels)
    for h in heads:
        oa_ref[rows, dv(h)] = _gla_post(outs[h], r_ref[rows, dv(h)].astype(_F32), g
                                        ).astype(oa_ref.dtype)
    return new_states


def _gla_kernel(qp_ref, kp_ref, vp_ref, rp_ref, lap_ref, qs_ref, ks_ref, vs_ref, rs_ref, las_ref,
                st_ref, g_ref, mcp_ref, mkp_ref, ocp_ref, rmp_ref, mcs_ref, mks_ref, ocs_ref, rms_ref,
                oap_ref, soutp_ref, oas_ref, souts_ref, s_scr, *, c, levels, n_sub, cs, per, levels_s):
    n = pl.program_id(1)

    @pl.when(n == 0)
    def _():
        s_scr[...] = jnp.zeros_like(s_scr)

    consts_p = _load_gla_consts(mcp_ref, mkp_ref, ocp_ref, rmp_ref)
    consts_s = _load_gla_consts(mcs_ref, mks_ref, ocs_ref, rms_ref)
    g = g_ref[...]
    heads = range(GLA_HEADS)
    for i in range(n_sub):
        new = _gla_rows(qp_ref, kp_ref, vp_ref, rp_ref, lap_ref, oap_ref, slice(i * c, (i + 1) * c),
                        [[s_scr[h]] for h in heads], consts_p, c, levels, g)
        for h in heads:
            s_scr[h] = new[h][0]
        if i == 0:
            new_s = _gla_rows(qs_ref, ks_ref, vs_ref, rs_ref, las_ref, oas_ref, slice(0, cs),
                              [[st_ref[e, h] for e in range(per)] for h in heads],
                              consts_s, cs, levels_s, g)
            for h in heads:
                for e in range(per):
                    souts_ref[e, h] = new_s[h][e]

    @pl.when(n == pl.num_programs(1) - 1)
    def _():
        soutp_ref[0] = s_scr[...]


def _gla(z, la, state, gla_g, batch, seq, dec_seq, c, rows_per_step):
    cs = 16
    per = cs // dec_seq
    dec_batch = state.shape[0]
    n_p = batch * seq
    steps = seq // rows_per_step
    n_sub = rows_per_step // c
    assert cs % dec_seq == 0 and batch * steps * per == dec_batch and n_p % cs == 0
    *consts_p, levels = _gla_consts(c)
    *consts_s, levels_s = _gla_consts(cs, per)
    kern = functools.partial(_gla_kernel, c=c, levels=levels, n_sub=n_sub, cs=cs, per=per,
                             levels_s=levels_s)
    rb = lambda b, n: b * steps + n
    sb = lambda b, n: n_p // cs + rb(b, n)
    return pl.pallas_call(
        kern,
        grid=(batch, steps),
        in_specs=[
            pl.BlockSpec((rows_per_step, Q_W), lambda b, n: (rb(b, n), Z_Q // Q_W)),
            pl.BlockSpec((rows_per_step, Q_W), lambda b, n: (rb(b, n), Z_K // Q_W)),
            pl.BlockSpec((rows_per_step, V_W), lambda b, n: (rb(b, n), Z_V // V_W)),
            pl.BlockSpec((rows_per_step, V_W), lambda b, n: (rb(b, n), Z_R // V_W)),
            pl.BlockSpec((rows_per_step, Q_W), lambda b, n: (rb(b, n), 0)),
            pl.BlockSpec((cs, Q_W), lambda b, n: (sb(b, n), Z_Q // Q_W)),
            pl.BlockSpec((cs, Q_W), lambda b, n: (sb(b, n), Z_K // Q_W)),
            pl.BlockSpec((cs, V_W), lambda b, n: (sb(b, n), Z_V // V_W)),
            pl.BlockSpec((cs, V_W), lambda b, n: (sb(b, n), Z_R // V_W)),
            pl.BlockSpec((cs, Q_W), lambda b, n: (sb(b, n), 0)),
            pl.BlockSpec((per, GLA_HEADS, GLA_DK, GLA_DV), lambda b, n: (rb(b, n), 0, 0, 0)),
            pl.BlockSpec((1, GLA_DV), lambda b, n: (0, 0)),
        ] + _const_specs(consts_p) + _const_specs(consts_s),
        out_specs=[
            pl.BlockSpec((rows_per_step, V_W), lambda b, n: (rb(b, n), 0)),
            pl.BlockSpec((1, GLA_HEADS, GLA_DK, GLA_DV), lambda b, n: (b, 0, 0, 0)),
            pl.BlockSpec((cs, V_W), lambda b, n: (rb(b, n), 0)),
            pl.BlockSpec((per, GLA_HEADS, GLA_DK, GLA_DV), lambda b, n: (rb(b, n), 0, 0, 0)),
        ],
        out_shape=[
            jax.ShapeDtypeStruct((n_p, V_W), _BF16),
            jax.ShapeDtypeStruct((batch, GLA_HEADS, GLA_DK, GLA_DV), _F32),
            jax.ShapeDtypeStruct((dec_batch * dec_seq, V_W), _BF16),
            jax.ShapeDtypeStruct((dec_batch, GLA_HEADS, GLA_DK, GLA_DV), _F32),
        ],
        scratch_shapes=[pltpu.VMEM((GLA_HEADS, GLA_DK, GLA_DV), _F32)],
        compiler_params=_params("arbitrary", "arbitrary"),
        name="gla",
    )(z, z, z, z, la, z, z, z, z, la, state, gla_g, *consts_p, *consts_s)


def _window_sums(ext_ref, t0, tt, lead):
    axis = len(lead)
    outs = []
    for gi, w in enumerate(POOL_WINDOWS):
        assert w & (w - 1) == 0 and w - 1 <= HALO
        lanes = slice(gi * POOL_GROUP_W, (gi + 1) * POOL_GROUP_W)
        s = ext_ref[lead + (slice(t0 - HALO, t0 + tt), lanes)]
        span = 1
        while span < w:
            s = s + pltpu.roll(s, span, axis=axis)
            span *= 2
        outs.append(s[lead + (slice(HALO, HALO + tt),)])
    return outs


def _pool_prompt_rows(ext_scr, pos0, r0, nr):
    sums = _window_sums(ext_scr, HALO + r0, nr, ())
    pos = (pos0 + r0 + 1 + lax.broadcasted_iota(jnp.int32, (nr, 1), 0)).astype(_F32)
    ps = []
    for gi, w in enumerate(POOL_WINDOWS):
        lanes = slice(gi * POOL_GROUP_W, (gi + 1) * POOL_GROUP_W)
        ps.append(sums[gi] / jnp.minimum(pos, float(w)) - ext_scr[HALO + r0:HALO + r0 + nr, lanes])
    return ps


def _pool_sample_tile(u_ref, buf_ref, nb_ref, ext_scr, bb, seq):
    u = u_ref[...].reshape(bb, seq, POOL_WIDTH)
    ext_scr[:, 0:HALO - POOL_BUF, :] = jnp.zeros((bb, HALO - POOL_BUF, POOL_WIDTH), _F32)
    ext_scr[:, HALO - POOL_BUF:HALO, :] = buf_ref[...]
    ext_scr[:, HALO:HALO + seq, :] = u
    sums = _window_sums(ext_scr, HALO, seq, (slice(None),))
    ps = []
    for gi, w in enumerate(POOL_WINDOWS):
        lanes = slice(gi * POOL_GROUP_W, (gi + 1) * POOL_GROUP_W)
        ps.append((sums[gi] / float(w) - u[:, :, lanes]).reshape(bb * seq, POOL_GROUP_W))
    nb_ref[...] = ext_scr[:, HALO + seq - POOL_BUF:HALO + seq, :]
    return ps


def _merge_core(rows, x_ref, oa_ref, ps, ga_ref, gb_ref, pw_ref, ps_ref, wo_ref, g2_ref,
                x1_ref, h2_ref):
    ob = jnp.concatenate([_dot(ps[gi].astype(_BF16), pw_ref[gi]) for gi in range(POOL_GROUPS)],
                         axis=1) * ps_ref[...]
    m = (jax.nn.sigmoid(ga_ref[rows, :].astype(_F32)) * oa_ref[rows, :].astype(_F32)
         + jax.nn.sigmoid(gb_ref[rows, :].astype(_F32)) * ob)
    x1 = x_ref[rows, :] + _dot(m.astype(_BF16), wo_ref[...])
    x1_ref[rows, :] = x1
    h2_ref[rows, :] = (x1 * _rms_scale(x1) * g2_ref[...]).astype(_BF16)


def _merge_prompt_kernel(x_ref, oa_ref, u_ref, ga_ref, gb_ref, pw_ref, ps_ref, wo_ref, g2_ref,
                         x1_ref, h2_ref, ext_scr, *, tt, sub):
    n = pl.program_id(1)

    @pl.when(n == 0)
    def _():
        ext_scr[0:HALO, :] = jnp.zeros((HALO, POOL_WIDTH), _F32)

    ext_scr[HALO:HALO + tt, :] = u_ref[...]
    for r0 in range(0, tt, sub):
        ps = _pool_prompt_rows(ext_scr, n * tt, r0, sub)
        _merge_core(slice(r0, r0 + sub), x_ref, oa_ref, ps, ga_ref, gb_ref, pw_ref, ps_ref, wo_ref,
                    g2_ref, x1_ref, h2_ref)
    ext_scr[0:HALO, :] = ext_scr[tt:tt + HALO, :]


def _merge_sample_kernel(x_ref, oa_ref, u_ref, buf_ref, ga_ref, gb_ref, pw_ref, ps_ref, wo_ref,
                         g2_ref, x1_ref, h2_ref, nb_ref, ext_scr, *, bb, seq):
    ps = _pool_sample_tile(u_ref, buf_ref, nb_ref, ext_scr, bb, seq)
    _merge_core(slice(None), x_ref, oa_ref, ps, ga_ref, gb_ref, pw_ref, ps_ref, wo_ref, g2_ref,
                x1_ref, h2_ref)


def _weight_specs():
    zeros = lambda nd: (lambda *idx: (0,) * nd)
    once = pl.Buffered(1)
    return [
        pl.BlockSpec((POOL_GROUPS, POOL_GROUP_W, POOL_OUT_GROUP_W), zeros(3), pipeline_mode=once),
        pl.BlockSpec((1, D_MODEL), zeros(2)),
        pl.BlockSpec((D_MODEL, D_MODEL), zeros(2), pipeline_mode=once),
        pl.BlockSpec((1, D_MODEL), zeros(2)),
    ]


def _merge_prompt(x, oa, u, zg, pool_w, pool_scale, w_out, g2, batch, seq, tt):
    steps = seq // tt
    rb = lambda b, n: b * steps + n
    return pl.pallas_call(
        functools.partial(_merge_prompt_kernel, tt=tt, sub=256),
        grid=(batch, steps),
        in_specs=[
            pl.BlockSpec((tt, D_MODEL), lambda b, n: (rb(b, n), 0)),
            pl.BlockSpec((tt, D_MODEL), lambda b, n: (rb(b, n), 0)),
            pl.BlockSpec((tt, POOL_WIDTH), lambda b, n: (rb(b, n), 0)),
            pl.BlockSpec((tt, D_MODEL), lambda b, n: (rb(b, n), 0)),
            pl.BlockSpec((tt, D_MODEL), lambda b, n: (rb(b, n), 1)),
        ] + _weight_specs(),
        out_specs=[
            pl.BlockSpec((tt, D_MODEL), lambda b, n: (rb(b, n), 0)),
            pl.BlockSpec((tt, D_MODEL), lambda b, n: (rb(b, n), 0)),
        ],
        out_shape=[
            jax.ShapeDtypeStruct((batch * seq, D_MODEL), _F32),
            jax.ShapeDtypeStruct((batch * seq, D_MODEL), _BF16),
        ],
        scratch_shapes=[pltpu.VMEM((HALO + tt, POOL_WIDTH), _F32)],
        compiler_params=_params("arbitrary", "arbitrary"),
        name="merge_prompt",
    )(x, oa, u, zg, zg, pool_w, pool_scale, w_out, g2)


def _merge_sample(x, oa, u, buf, zg, row0, pool_w, pool_scale, w_out, g2, seq, bb):
    batch = buf.shape[0]
    rows = bb * seq
    rb0 = row0 // rows
    assert PAST_LEN >= max(POOL_WINDOWS) and row0 % rows == 0 and seq == 8
    return pl.pallas_call(
        functools.partial(_merge_sample_kernel, bb=bb, seq=seq),
        grid=(batch // bb,),
        in_specs=[
            pl.BlockSpec((rows, D_MODEL), lambda i: (i, 0)),
            pl.BlockSpec((rows, D_MODEL), lambda i: (i, 0)),
            pl.BlockSpec((rows, POOL_WIDTH), lambda i: (rb0 + i, 0)),
            pl.BlockSpec((bb, POOL_BUF, POOL_WIDTH), lambda i: (i, 0, 0)),
            pl.BlockSpec((rows, D_MODEL), lambda i: (rb0 + i, 0)),
            pl.BlockSpec((rows, D_MODEL), lambda i: (rb0 + i, 1)),
        ] + _weight_specs(),
        out_specs=[
            pl.BlockSpec((rows, D_MODEL), lambda i: (i, 0)),
            pl.BlockSpec((rows, D_MODEL), lambda i: (i, 0)),
            pl.BlockSpec((bb, POOL_BUF, POOL_WIDTH), lambda i: (i, 0, 0)),
        ],
        out_shape=[
            jax.ShapeDtypeStruct((batch * seq, D_MODEL), _F32),
            jax.ShapeDtypeStruct((batch * seq, D_MODEL), _BF16),
            jax.ShapeDtypeStruct((batch, POOL_BUF, POOL_WIDTH), _F32),
        ],
        scratch_shapes=[pltpu.VMEM((bb, HALO + seq, POOL_WIDTH), _F32)],
        compiler_params=_params("arbitrary"),
        name="merge_sample",
    )(x, oa, u, buf, zg, zg, pool_w, pool_scale, w_out, g2)


def _mlp_kernel(h2_ref, x1_hbm, wu_ref, wd_ref, g_ref, y_ref, x1_scr, x1_sem):
    i = pl.program_id(0)
    f = pl.program_id(1)
    tm = y_ref.shape[0]

    def x1_copy():
        return pltpu.make_async_copy(x1_hbm.at[pl.ds(pl.multiple_of(i * tm, tm), tm), :],
                                     x1_scr, x1_sem)

    @pl.when(f == 0)
    def _():
        x1_copy().start()
        y_ref[...] = jnp.zeros_like(y_ref)

    a = jnp.maximum(_dot(h2_ref[...], wu_ref[...]), 0.0)
    y_ref[...] += _dot((a * a).astype(_BF16), wd_ref[...])

    @pl.when(f == pl.num_programs(1) - 1)
    def _():
        x1_copy().wait()
        x2 = y_ref[...] + x1_scr[...]
        y_ref[...] = x2 * _rms_scale(x2) * g_ref[...]


def _mlp(h2, x1, w_up, w_down, g, tm, tf):
    m = h2.shape[0]
    assert m % tm == 0
    return pl.pallas_call(
        _mlp_kernel,
        grid=(m // tm, D_FF // tf),
        in_specs=[
            pl.BlockSpec((tm, D_MODEL), lambda i, f: (i, 0)),
            pl.BlockSpec(memory_space=pl.ANY),
            pl.BlockSpec((D_MODEL, tf), lambda i, f: (0, f)),
            pl.BlockSpec((tf, D_MODEL), lambda i, f: (f, 0)),
            pl.BlockSpec((1, D_MODEL), lambda i, f: (0, 0)),
        ],
        out_specs=pl.BlockSpec((tm, D_MODEL), lambda i, f: (i, 0)),
        out_shape=jax.ShapeDtypeStruct((m, D_MODEL), _F32),
        scratch_shapes=[pltpu.VMEM((tm, D_MODEL), _F32), pltpu.SemaphoreType.DMA(())],
        compiler_params=_params("arbitrary", "arbitrary"),
        name="mlp",
    )(h2, x1, w_up, w_down, g)


def kernel(x_prompt, x_sample, state_gla, state_pool, norm_mix_g, w_in, w_alpha_up, b_alpha,
           gla_norm_g, pool_w, pool_scale, w_out, norm_mlp_g, w_up, w_down, norm_final_g):
    batch, seq, _ = x_prompt.shape
    dec_batch, dec_seq, _ = x_sample.shape
    n_p = batch * seq
    n_s = dec_batch * dec_seq
    xp = x_prompt.reshape(n_p, D_MODEL)
    xs = x_sample.reshape(n_s, D_MODEL)
    row = lambda v: v.reshape(1, -1).astype(_F32)

    wup_pad = jnp.pad(w_alpha_up, ((0, LANES - GATE_RANK), (0, 0))).astype(_BF16)
    wup3 = jnp.concatenate([wup_pad] * 3, axis=0)
    pool_w_b, w_out_b = pool_w.astype(_BF16), w_out.astype(_BF16)
    wt = w_in.T

    h, la, u = _norm_gate_u(xp, xs, row(norm_mix_g), wt, wup3, row(b_alpha), tm=512)
    z, w_up_b = _proj(h, wt, 0, QKVR_W // PROJ_TN, _BF16, 1024, w_up, (512, 1024))
    zg, w_down_b = _proj(h, wt, GA_OFF, 2 * D_MODEL // PROJ_TN, _BF16, 1024, w_down, (256, D_MODEL))

    oa_p, s_gla_p, oa_s, s_gla_s = _gla(z, la, state_gla, row(gla_norm_g), batch, seq, dec_seq,
                                        c=64, rows_per_step=128)
    buf_p = jnp.stack([u[(b + 1) * seq - POOL_BUF:(b + 1) * seq] for b in range(batch)], axis=0)

    merge_w = (pool_w_b, row(pool_scale), w_out_b, row(norm_mlp_g))
    x1_p, h2_p = _merge_prompt(xp, oa_p, u, zg, *merge_w, batch, seq, tt=512)
    x1_s, h2_s, buf_s = _merge_sample(xs, oa_s, u, state_pool, zg, n_p, *merge_w, dec_seq, bb=32)
    mlp_w = (w_up_b, w_down_b, row(norm_final_g))
    y_p = _mlp(h2_p, x1_p, *mlp_w, tm=1024, tf=512).reshape(batch, seq, D_MODEL)
    y_s = _mlp(h2_s, x1_s, *mlp_w, tm=1024, tf=512).reshape(dec_batch, dec_seq, D_MODEL)
    return (y_p, y_s, s_gla_p, buf_p, s_gla_s, buf_s)
```

```python
import functools

import jax
import jax.numpy as jnp
import numpy as np
from jax import lax
from jax.experimental import pallas as pl
from jax.experimental.pallas import tpu as pltpu

D_MODEL = 2048
GLA_HEADS = 4
GLA_DK = 256
GLA_DV = 512
GATE_RANK = 16
GATE_TAU = 16.0
POOL_WIDTH = 1024
POOL_GROUPS = 4
POOL_GROUP_W = 256
POOL_OUT_GROUP_W = 512
POOL_WINDOWS = (2, 4, 8, 16)
POOL_BUF = 15
D_FF = 8192
EPS = 1e-6
PAST_LEN = 16384

Q_W = GLA_HEADS * GLA_DK
V_W = GLA_HEADS * GLA_DV
QKVR_W = 2 * Q_W + 2 * V_W
A_OFF = QKVR_W
U_OFF = A_OFF + GATE_RANK
GA_OFF = U_OFF + POOL_WIDTH
Z_Q, Z_K, Z_V, Z_R = 0, Q_W, 2 * Q_W, 2 * Q_W + V_W
Z_GA, Z_GB = QKVR_W, QKVR_W + D_MODEL

LANES = 128
PROJ_TN = 1024
HALO = 16
VMEM_LIMIT = 56 * 1024 * 1024

_F32 = jnp.float32
_BF16 = jnp.bfloat16


def _dot(a, b):
    return jnp.dot(a, b, preferred_element_type=_F32)


def _dot_nt(a, b):
    return lax.dot_general(a, b, (((1,), (1,)), ((), ())), preferred_element_type=_F32)


def _dot_tn(a, b):
    return lax.dot_general(a, b, (((0,), (0,)), ((), ())), preferred_element_type=_F32)


def _split3(x):
    hi = x.astype(_BF16)
    r1 = x - hi.astype(_F32)
    mid = r1.astype(_BF16)
    lo = (r1 - mid.astype(_F32)).astype(_BF16)
    return hi, mid, lo


def _rms_scale(x):
    return lax.rsqrt(jnp.mean(x * x, axis=-1, keepdims=True) + EPS)


def _params(*sem):
    return pltpu.CompilerParams(dimension_semantics=sem, vmem_limit_bytes=VMEM_LIMIT)


def _norm_gate_u_kernel(xp_ref, xs_ref, g_ref, wa_ref, wup_ref, ba_ref, wu_ref, wo_ref, pw_ref,
                        h_ref, la_ref, u_ref, wo_out_ref, pw_out_ref, wu_scr,
                        *, prompt_tiles, sub, n_side):
    i = pl.program_id(0)
    rc = 256

    @pl.when(i == 0)
    def _():
        for r in range(0, POOL_WIDTH, rc):
            wu_scr[r:r + rc, :] = wu_ref[r:r + rc, :].astype(_BF16)
        pw_out_ref[...] = pw_ref[...].astype(_BF16)

    @pl.when(i < n_side)
    def _():
        wo_out_ref[...] = wo_ref[...].astype(_BF16)

    def compute(x_ref):
        wa = wa_ref[...].astype(_BF16)
        for r0 in range(0, x_ref.shape[0], sub):
            rows = slice(r0, r0 + sub)
            x = x_ref[rows, :]
            hb = (x * _rms_scale(x) * g_ref[...]).astype(_BF16)
            h_ref[rows, :] = hb
            u_ref[rows, :] = _dot_nt(hb, wu_scr[...])
            a_lr = _dot_nt(hb, wa)
            a_hi, a_mid, a_lo = _split3(a_lr)
            a3 = jnp.concatenate([a_hi, a_mid, a_lo], axis=1)
            a_logit = _dot(a3, wup_ref[...]) + ba_ref[...]
            ls = jnp.minimum(a_logit, 0.0) - jnp.log1p(jnp.exp(-jnp.abs(a_logit)))
            la_ref[rows, :] = ls * (1.0 / GATE_TAU)

    @pl.when(i < prompt_tiles)
    def _():
        compute(xp_ref)

    @pl.when(i >= prompt_tiles)
    def _():
        compute(xs_ref)


def _norm_gate_u(xp, xs, g, wt, wup3, b_alpha, w_out, pool_w, tm):
    n_p, n_s = xp.shape[0], xs.shape[0]
    pt, st = n_p // tm, n_s // tm
    m = n_p + n_s
    const = lambda i: (0, 0)
    wo_rows = 128
    n_side = w_out.shape[0] // wo_rows
    assert n_side <= pt + st and n_side * wo_rows == w_out.shape[0]
    wo_spec = pl.BlockSpec((wo_rows, D_MODEL), lambda i: (jnp.minimum(i, n_side - 1), 0))
    pw_spec = pl.BlockSpec(pool_w.shape, lambda i: (0, 0, 0))
    return pl.pallas_call(
        functools.partial(_norm_gate_u_kernel, prompt_tiles=pt, sub=min(tm, 256), n_side=n_side),
        grid=(pt + st,),
        in_specs=[
            pl.BlockSpec((tm, D_MODEL), lambda i: (jnp.minimum(i, pt - 1), 0)),
            pl.BlockSpec((tm, D_MODEL), lambda i: (jnp.maximum(i - pt, 0), 0)),
            pl.BlockSpec((1, D_MODEL), const),
            pl.BlockSpec((LANES, D_MODEL), lambda i: (A_OFF // LANES, 0)),
            pl.BlockSpec((3 * LANES, Q_W), const),
            pl.BlockSpec((1, Q_W), const),
            pl.BlockSpec((pl.Element(POOL_WIDTH), pl.Element(D_MODEL)), lambda i: (U_OFF, 0),
                         pipeline_mode=pl.Buffered(1)),
            wo_spec,
            pw_spec,
        ],
        out_specs=[
            pl.BlockSpec((tm, D_MODEL), lambda i: (i, 0)),
            pl.BlockSpec((tm, Q_W), lambda i: (i, 0)),
            pl.BlockSpec((tm, POOL_WIDTH), lambda i: (i, 0)),
            wo_spec,
            pw_spec,
        ],
        out_shape=[
            jax.ShapeDtypeStruct((m, D_MODEL), _BF16),
            jax.ShapeDtypeStruct((m, Q_W), _F32),
            jax.ShapeDtypeStruct((m, POOL_WIDTH), _F32),
            jax.ShapeDtypeStruct(w_out.shape, _BF16),
            jax.ShapeDtypeStruct(pool_w.shape, _BF16),
        ],
        scratch_shapes=[pltpu.VMEM((POOL_WIDTH, D_MODEL), _BF16)],
        compiler_params=_params("arbitrary"),
        name="norm_gate_u",
    )(xp, xs, g, wt, wup3, b_alpha, wt, w_out, pool_w)


def _proj_kernel(h_ref, wt_ref, *refs, side_steps):
    n_s = len(side_steps)
    side_refs, z_ref, side_out_refs, w_scr = refs[:n_s], refs[n_s], refs[n_s + 1:-1], refs[-1]
    rc = 256

    @pl.when(pl.program_id(1) == 0)
    def _():
        for r in range(0, w_scr.shape[0], rc):
            w_scr[r:r + rc, :] = wt_ref[r:r + rc, :].astype(_BF16)

    z_ref[...] = _dot_nt(h_ref[...], w_scr[...]).astype(z_ref.dtype)

    step = pl.program_id(0) * pl.num_programs(1) + pl.program_id(1)
    for (lo, hi), src, dst in zip(side_steps, side_refs, side_out_refs):
        @pl.when((step >= lo) & (step < hi))
        def _(src=src, dst=dst):
            dst[...] = src[...].astype(_BF16)


def _proj(h, wt, pieces, out_dtype, tm, sides, tn=PROJ_TN):
    m = h.shape[0]
    (col0, n0), (col1, n1) = pieces
    n_tiles = n0 + n1
    n_i = m // tm
    assert all(c % 8 == 0 and c + n * tn <= wt.shape[0] for c, n in pieces)

    side_specs, side_steps, first = [], [], 0
    for arr, blk in sides:
        sr, sc = arr.shape[0] // blk[0], arr.shape[1] // blk[1]
        assert sr * blk[0] == arr.shape[0] and sc * blk[1] == arr.shape[1]
        n_blk = sr * sc

        def side_idx(j, i, first=first, n_blk=n_blk, sc=sc):
            b = jnp.clip(j * n_i + i - first, 0, n_blk - 1)
            return (b // sc, b % sc)

        side_specs.append(pl.BlockSpec(blk, side_idx))
        side_steps.append((first, first + n_blk))
        first += n_blk
    assert first <= n_tiles * n_i

    def w_idx(j, i):
        start = jnp.where(j < n0, col0 + j * tn, col1 + (j - n0) * tn)
        return (pl.multiple_of(start, 8), 0)

    return pl.pallas_call(
        functools.partial(_proj_kernel, side_steps=tuple(side_steps)),
        grid=(n_tiles, n_i),
        in_specs=[
            pl.BlockSpec((tm, D_MODEL), lambda j, i: (i, 0)),
            pl.BlockSpec((pl.Element(tn), pl.Element(D_MODEL)), w_idx),
        ] + side_specs,
        out_specs=[pl.BlockSpec((tm, tn), lambda j, i: (i, j))] + side_specs,
        out_shape=[jax.ShapeDtypeStruct((m, n_tiles * tn), out_dtype)]
        + [jax.ShapeDtypeStruct(arr.shape, _BF16) for arr, _ in sides],
        scratch_shapes=[pltpu.VMEM((tn, D_MODEL), _BF16)],
        compiler_params=_params("arbitrary", "arbitrary"),
        name="proj",
    )(h, wt, *[arr for arr, _ in sides])


def _gla_consts(c, per=1):
    g = c // per
    levels = int(np.log2(g))
    assert 2 ** levels == g and g * per == c
    t = np.arange(c)
    seq_id = t // g
    mats = [np.tril(np.ones((c, c), np.float32)) * (seq_id[:, None] == seq_id[None, :])]
    masks = [np.eye(c, dtype=np.float32)]
    for l in range(levels):
        h = 2 ** l
        blk = t // (2 * h)
        mid = blk * 2 * h + h - 1
        upper = (t // h) % 2 == 1
        m = np.zeros((c, c), np.float32)
        for row in range(c):
            if upper[row]:
                m[row, mid[row] + 1: row + 1] = 1.0
            else:
                m[row, row + 1: mid[row] + 1] = 1.0
        mats.append(m)
        masks.append(((blk[:, None] == blk[None, :]) & upper[:, None] & (~upper)[None, :])
                     .astype(np.float32))
    m_all = np.concatenate(mats, axis=0)
    m_cat = np.concatenate([m_all, m_all, m_all], axis=1)
    rowmasks = np.stack([(seq_id == e).astype(np.float32)[:, None] for e in range(per)])
    ones_cols = np.stack([np.tile(np.broadcast_to(rowmasks[e], (c, LANES)), (3, 1))
                          for e in range(per)])
    return (jnp.asarray(m_cat, _BF16), jnp.asarray(np.stack(masks), _F32),
            jnp.asarray(ones_cols, _BF16), jnp.asarray(rowmasks, _F32), levels)


def _gla_post(o, r, g):
    o_n = o * _rms_scale(o) * g
    return o_n * (r * jax.nn.sigmoid(r))


def _gla_rows(q_ref, k_ref, v_ref, r_ref, la_ref, oa_ref, rows, load_state, store_state, consts,
              c, levels, g):
    m_cat, masks, ones_cols, rowmasks = consts
    heads = range(GLA_HEADS)
    per = len(ones_cols)
    gsz = c // per
    dk = lambda h: slice(h * GLA_DK, (h + 1) * GLA_DK)
    dv = lambda h: slice(h * GLA_DV, (h + 1) * GLA_DV)
    sl = lambda a, h: a[:, dk(h)]
    qs = [q_ref[rows, dk(h)].astype(_F32) * (GLA_DK ** -0.5) for h in heads]
    ks = [k_ref[rows, dk(h)].astype(_F32) for h in heads]
    vbs = [v_ref[rows, dv(h)] for h in heads]
    hi, mid, lo = _split3(la_ref[rows, :])
    la3 = jnp.concatenate([hi, mid, lo], axis=0)
    eb = _dot(m_cat, la3)
    dcols = [jnp.exp(_dot_tn(la3, ones_cols[e])) for e in range(per)]
    b = eb[0:c]
    scores = [jnp.where(masks[0] > 0, _dot_nt(qs[h].astype(_BF16), ks[h].astype(_BF16)), 0.0)
              for h in heads]
    for l in range(levels):
        x = jnp.exp(eb[(l + 1) * c:(l + 2) * c])
        for h in heads:
            xh = sl(x, h)
            s_l = _dot_nt((qs[h] * xh).astype(_BF16), (ks[h] * xh).astype(_BF16))
            scores[h] = jnp.where(masks[l + 1] > 0, s_l, scores[h])
    eb0 = jnp.exp(b)
    ebl = jnp.exp(jnp.concatenate(
        [b[(e + 1) * gsz - 1:(e + 1) * gsz, :] - b[e * gsz:(e + 1) * gsz, :] for e in range(per)],
        axis=0))
    pick = (lambda a, e: a) if per == 1 else (lambda a, e: a * rowmasks[e])
    states = [[load_state(h, e) for e in range(per)] for h in heads]
    outs = []
    for h in heads:
        o = _dot(scores[h].astype(_BF16), vbs[h])
        qd = qs[h] * sl(eb0, h)
        for e in range(per):
            o = o + _dot(pick(qd, e).astype(_BF16), states[h][e].astype(_BF16))
        outs.append(o)
    for h in heads:
        kd = ks[h] * sl(ebl, h)
        for e in range(per):
            dh = dcols[e][dk(h)]
            decay = jnp.concatenate([dh] * (GLA_DV // LANES), axis=1)
            store_state(h, e, decay * states[h][e] + _dot_tn(pick(kd, e).astype(_BF16), vbs[h]))
    for h in heads:
        oa_ref[rows, dv(h)] = _gla_post(outs[h], r_ref[rows, dv(h)].astype(_F32), g
                                        ).astype(oa_ref.dtype)


def _load_gla_consts(mc_ref, mk_ref, oc_ref, rm_ref):
    return (mc_ref[...], [mk_ref[i] for i in range(mk_ref.shape[0])],
            [oc_ref[e] for e in range(oc_ref.shape[0])], [rm_ref[e] for e in range(rm_ref.shape[0])])


def _const_specs(arrays):
    return [pl.BlockSpec(a.shape, lambda *idx, nd=a.ndim: (0,) * nd) for a in arrays]


def _gla_kernel(qp_ref, kp_ref, vp_ref, rp_ref, lap_ref, qs_ref, ks_ref, vs_ref, rs_ref, las_ref,
                st_ref, g_ref, mcp_ref, mkp_ref, ocp_ref, rmp_ref, mcs_ref, mks_ref, ocs_ref, rms_ref,
                oap_ref, soutp_ref, oas_ref, souts_ref, s_scr, *, c, levels, n_sub, cs, per, levels_s,
                n_slabs):
    n = pl.program_id(1)

    @pl.when(n == 0)
    def _():
        s_scr[...] = jnp.zeros_like(s_scr)

    consts_p = _load_gla_consts(mcp_ref, mkp_ref, ocp_ref, rmp_ref)
    consts_s = _load_gla_consts(mcs_ref, mks_ref, ocs_ref, rms_ref)
    g = g_ref[...]

    def prompt_load(h, e):
        return s_scr[h]

    def prompt_store(h, e, value):
        s_scr[h] = value

    every = n_sub // n_slabs
    for i in range(n_sub):
        _gla_rows(qp_ref, kp_ref, vp_ref, rp_ref, lap_ref, oap_ref, slice(i * c, (i + 1) * c),
                  prompt_load, prompt_store, consts_p, c, levels, g)
        if i % every == 0:
            j = i // every

            def slab_load(h, e, j=j):
                return st_ref[j * per + e, h]

            def slab_store(h, e, value, j=j):
                souts_ref[j * per + e, h] = value

            _gla_rows(qs_ref, ks_ref, vs_ref, rs_ref, las_ref, oas_ref, slice(j * cs, (j + 1) * cs),
                      slab_load, slab_store, consts_s, cs, levels_s, g)

    @pl.when(n == pl.num_programs(1) - 1)
    def _():
        soutp_ref[0] = s_scr[...]


def _gla(z, la, state, gla_g, batch, seq, dec_seq, c, rows_per_step):
    cs = 16
    per = cs // dec_seq
    dec_batch = state.shape[0]
    n_p = batch * seq
    steps = seq // rows_per_step
    n_sub = rows_per_step // c
    n_slabs = dec_batch // per // (batch * steps)
    assert cs % dec_seq == 0 and batch * steps * n_slabs * per == dec_batch and n_p % cs == 0
    assert n_sub % n_slabs == 0
    srows, sseqs = n_slabs * cs, n_slabs * per
    *consts_p, levels = _gla_consts(c)
    *consts_s, levels_s = _gla_consts(cs, per)
    kern = functools.partial(_gla_kernel, c=c, levels=levels, n_sub=n_sub, cs=cs, per=per,
                             levels_s=levels_s, n_slabs=n_slabs)
    rb = lambda b, n: b * steps + n
    sb = lambda b, n: n_p // srows + rb(b, n)
    return pl.pallas_call(
        kern,
        grid=(batch, steps),
        in_specs=[
            pl.BlockSpec((rows_per_step, Q_W), lambda b, n: (rb(b, n), Z_Q // Q_W)),
            pl.BlockSpec((rows_per_step, Q_W), lambda b, n: (rb(b, n), Z_K // Q_W)),
            pl.BlockSpec((rows_per_step, V_W), lambda b, n: (rb(b, n), Z_V // V_W)),
            pl.BlockSpec((rows_per_step, V_W), lambda b, n: (rb(b, n), Z_R // V_W)),
            pl.BlockSpec((rows_per_step, Q_W), lambda b, n: (rb(b, n), 0)),
            pl.BlockSpec((srows, Q_W), lambda b, n: (sb(b, n), Z_Q // Q_W)),
            pl.BlockSpec((srows, Q_W), lambda b, n: (sb(b, n), Z_K // Q_W)),
            pl.BlockSpec((srows, V_W), lambda b, n: (sb(b, n), Z_V // V_W)),
            pl.BlockSpec((srows, V_W), lambda b, n: (sb(b, n), Z_R // V_W)),
            pl.BlockSpec((srows, Q_W), lambda b, n: (sb(b, n), 0)),
            pl.BlockSpec((sseqs, GLA_HEADS, GLA_DK, GLA_DV), lambda b, n: (rb(b, n), 0, 0, 0)),
            pl.BlockSpec((1, GLA_DV), lambda b, n: (0, 0)),
        ] + _const_specs(consts_p) + _const_specs(consts_s),
        out_specs=[
            pl.BlockSpec((rows_per_step, V_W), lambda b, n: (rb(b, n), 0)),
            pl.BlockSpec((1, GLA_HEADS, GLA_DK, GLA_DV), lambda b, n: (b, 0, 0, 0)),
            pl.BlockSpec((srows, V_W), lambda b, n: (rb(b, n), 0)),
            pl.BlockSpec((sseqs, GLA_HEADS, GLA_DK, GLA_DV), lambda b, n: (rb(b, n), 0, 0, 0)),
        ],
        out_shape=[
            jax.ShapeDtypeStruct((n_p, V_W), _BF16),
            jax.ShapeDtypeStruct((batch, GLA_HEADS, GLA_DK, GLA_DV), _F32),
            jax.ShapeDtypeStruct((dec_batch * dec_seq, V_W), _BF16),
            jax.ShapeDtypeStruct((dec_batch, GLA_HEADS, GLA_DK, GLA_DV), _F32),
        ],
        scratch_shapes=[pltpu.VMEM((GLA_HEADS, GLA_DK, GLA_DV), _F32)],
        compiler_params=_params("arbitrary", "arbitrary"),
        name="gla",
    )(z, z, z, z, la, z, z, z, z, la, state, gla_g, *consts_p, *consts_s)


def _window_sums(ext_ref, t0, tt, lead):
    axis = len(lead)
    outs = []
    for gi, w in enumerate(POOL_WINDOWS):
        assert w & (w - 1) == 0 and w - 1 <= HALO
        lanes = slice(gi * POOL_GROUP_W, (gi + 1) * POOL_GROUP_W)
        s = ext_ref[lead + (slice(t0 - HALO, t0 + tt), lanes)]
        span = 1
        while span < w:
            s = s + pltpu.roll(s, span, axis=axis)
            span *= 2
        outs.append(s[lead + (slice(HALO, HALO + tt),)])
    return outs


def _pool_prompt_rows(ext_scr, pos0, r0, nr):
    sums = _window_sums(ext_scr, HALO + r0, nr, ())
    pos = (pos0 + r0 + 1 + lax.broadcasted_iota(jnp.int32, (nr, 1), 0)).astype(_F32)
    ps = []
    for gi, w in enumerate(POOL_WINDOWS):
        lanes = slice(gi * POOL_GROUP_W, (gi + 1) * POOL_GROUP_W)
        ps.append(sums[gi] / jnp.minimum(pos, float(w)) - ext_scr[HALO + r0:HALO + r0 + nr, lanes])
    return ps


def _pool_sample_tile(u_ref, buf_ref, nb_ref, ext_scr, bb, seq):
    u = u_ref[...].reshape(bb, seq, POOL_WIDTH)
    ext_scr[:, 0:HALO - POOL_BUF, :] = jnp.zeros((bb, HALO - POOL_BUF, POOL_WIDTH), _F32)
    ext_scr[:, HALO - POOL_BUF:HALO, :] = buf_ref[...]
    ext_scr[:, HALO:HALO + seq, :] = u
    sums = _window_sums(ext_scr, HALO, seq, (slice(None),))
    ps = []
    for gi, w in enumerate(POOL_WINDOWS):
        lanes = slice(gi * POOL_GROUP_W, (gi + 1) * POOL_GROUP_W)
        ps.append((sums[gi] / float(w) - u[:, :, lanes]).reshape(bb * seq, POOL_GROUP_W))
    nb_ref[...] = ext_scr[:, HALO + seq - POOL_BUF:HALO + seq, :]
    return ps


def _merge_gate(rows, oa_ref, ps, ga_ref, gb_ref, pw_ref, ps_ref):
    ob = jnp.concatenate([_dot(ps[gi].astype(_BF16), pw_ref[gi]) for gi in range(POOL_GROUPS)],
                         axis=1) * ps_ref[...]
    m = (jax.nn.sigmoid(ga_ref[rows, :].astype(_F32)) * oa_ref[rows, :].astype(_F32)
         + jax.nn.sigmoid(gb_ref[rows, :].astype(_F32)) * ob)
    return m.astype(_BF16)


def _merge_project(rows, x_ref, m, wo_ref, g2_ref, x1_ref, h2_ref):
    x1 = x_ref[rows, :] + _dot(m, wo_ref[...])
    x1_ref[rows, :] = x1
    h2_ref[rows, :] = (x1 * _rms_scale(x1) * g2_ref[...]).astype(_BF16)


def _merge_core(rows, x_ref, oa_ref, ps, ga_ref, gb_ref, pw_ref, ps_ref, wo_ref, g2_ref,
                x1_ref, h2_ref):
    m = _merge_gate(rows, oa_ref, ps, ga_ref, gb_ref, pw_ref, ps_ref)
    _merge_project(rows, x_ref, m, wo_ref, g2_ref, x1_ref, h2_ref)


def _merge_prompt_kernel(x_ref, oa_ref, u_ref, ga_ref, gb_ref, pw_ref, ps_ref, wo_ref, g2_ref,
                         x1_ref, h2_ref, ext_scr, *, tt, sub):
    n = pl.program_id(1)

    @pl.when(n == 0)
    def _():
        ext_scr[0:HALO, :] = jnp.zeros((HALO, POOL_WIDTH), _F32)

    ext_scr[HALO:HALO + tt, :] = u_ref[...]
    for r0 in range(0, tt, sub):
        ps = _pool_prompt_rows(ext_scr, n * tt, r0, sub)
        _merge_core(slice(r0, r0 + sub), x_ref, oa_ref, ps, ga_ref, gb_ref, pw_ref, ps_ref, wo_ref,
                    g2_ref, x1_ref, h2_ref)
    ext_scr[0:HALO, :] = ext_scr[tt:tt + HALO, :]


def _merge_sample_kernel(x_ref, oa_ref, u_ref, buf_ref, ga_ref, gb_ref, pw_ref, ps_ref, wo_ref,
                         g2_ref, x1_ref, h2_ref, nb_ref, ext_scr, *, bb, seq):
    ps = _pool_sample_tile(u_ref, buf_ref, nb_ref, ext_scr, bb, seq)
    _merge_core(slice(None), x_ref, oa_ref, ps, ga_ref, gb_ref, pw_ref, ps_ref, wo_ref, g2_ref,
                x1_ref, h2_ref)


def _weight_specs():
    zeros = lambda nd: (lambda *idx: (0,) * nd)
    once = pl.Buffered(1)
    return [
        pl.BlockSpec((POOL_GROUPS, POOL_GROUP_W, POOL_OUT_GROUP_W), zeros(3), pipeline_mode=once),
        pl.BlockSpec((1, D_MODEL), zeros(2)),
        pl.BlockSpec((D_MODEL, D_MODEL), zeros(2), pipeline_mode=once),
        pl.BlockSpec((1, D_MODEL), zeros(2)),
    ]


def _merge_prompt(x, oa, u, zg, pool_w, pool_scale, w_out, g2, batch, seq, tt):
    steps = seq // tt
    rb = lambda b, n: b * steps + n
    return pl.pallas_call(
        functools.partial(_merge_prompt_kernel, tt=tt, sub=256),
        grid=(batch, steps),
        in_specs=[
            pl.BlockSpec((tt, D_MODEL), lambda b, n: (rb(b, n), 0)),
            pl.BlockSpec((tt, D_MODEL), lambda b, n: (rb(b, n), 0)),
            pl.BlockSpec((tt, POOL_WIDTH), lambda b, n: (rb(b, n), 0)),
            pl.BlockSpec((tt, D_MODEL), lambda b, n: (rb(b, n), Z_GA // D_MODEL)),
            pl.BlockSpec((tt, D_MODEL), lambda b, n: (rb(b, n), Z_GB // D_MODEL)),
        ] + _weight_specs(),
        out_specs=[
            pl.BlockSpec((tt, D_MODEL), lambda b, n: (rb(b, n), 0)),
            pl.BlockSpec((tt, D_MODEL), lambda b, n: (rb(b, n), 0)),
        ],
        out_shape=[
            jax.ShapeDtypeStruct((batch * seq, D_MODEL), _F32),
            jax.ShapeDtypeStruct((batch * seq, D_MODEL), _BF16),
        ],
        scratch_shapes=[pltpu.VMEM((HALO + tt, POOL_WIDTH), _F32)],
        compiler_params=_params("arbitrary", "arbitrary"),
        name="merge_prompt",
    )(x, oa, u, zg, zg, pool_w, pool_scale, w_out, g2)


def _merge_sample(x, oa, u, buf, zg, row0, pool_w, pool_scale, w_out, g2, seq, bb):
    batch = buf.shape[0]
    rows = bb * seq
    rb0 = row0 // rows
    assert PAST_LEN >= max(POOL_WINDOWS) and row0 % rows == 0 and seq == 8
    return pl.pallas_call(
        functools.partial(_merge_sample_kernel, bb=bb, seq=seq),
        grid=(batch // bb,),
        in_specs=[
            pl.BlockSpec((rows, D_MODEL), lambda i: (i, 0)),
            pl.BlockSpec((rows, D_MODEL), lambda i: (i, 0)),
            pl.BlockSpec((rows, POOL_WIDTH), lambda i: (rb0 + i, 0)),
            pl.BlockSpec((bb, POOL_BUF, POOL_WIDTH), lambda i: (i, 0, 0)),
            pl.BlockSpec((rows, D_MODEL), lambda i: (rb0 + i, Z_GA // D_MODEL)),
            pl.BlockSpec((rows, D_MODEL), lambda i: (rb0 + i, Z_GB // D_MODEL)),
        ] + _weight_specs(),
        out_specs=[
            pl.BlockSpec((rows, D_MODEL), lambda i: (i, 0)),
            pl.BlockSpec((rows, D_MODEL), lambda i: (i, 0)),
            pl.BlockSpec((bb, POOL_BUF, POOL_WIDTH), lambda i: (i, 0, 0)),
        ],
        out_shape=[
            jax.ShapeDtypeStruct((batch * seq, D_MODEL), _F32),
            jax.ShapeDtypeStruct((batch * seq, D_MODEL), _BF16),
            jax.ShapeDtypeStruct((batch, POOL_BUF, POOL_WIDTH), _F32),
        ],
        scratch_shapes=[pltpu.VMEM((bb, HALO + seq, POOL_WIDTH), _F32)],
        compiler_params=_params("arbitrary"),
        name="merge_sample",
    )(x, oa, u, buf, zg, zg, pool_w, pool_scale, w_out, g2)


def _mlp_kernel(h2_ref, x1_hbm, wu_ref, wd_ref, g_ref, y_ref, x1_scr, x1_sem):
    i = pl.program_id(0)
    f = pl.program_id(1)
    tm = y_ref.shape[0]

    def x1_copy():
        return pltpu.make_async_copy(x1_hbm.at[pl.ds(pl.multiple_of(i * tm, tm), tm), :],
                                     x1_scr, x1_sem)

    @pl.when(f == 0)
    def _():
        x1_copy().start()
        y_ref[...] = jnp.zeros_like(y_ref)

    a = jnp.maximum(_dot(h2_ref[...], wu_ref[...]), 0.0)
    y_ref[...] += _dot((a * a).astype(_BF16), wd_ref[...])

    @pl.when(f == pl.num_programs(1) - 1)
    def _():
        x1_copy().wait()
        x2 = y_ref[...] + x1_scr[...]
        y_ref[...] = x2 * _rms_scale(x2) * g_ref[...]


def _mlp(h2, x1, w_up, w_down, g, tm, tf):
    m = h2.shape[0]
    assert m % tm == 0
    return pl.pallas_call(
        _mlp_kernel,
        grid=(m // tm, D_FF // tf),
        in_specs=[
            pl.BlockSpec((tm, D_MODEL), lambda i, f: (i, 0)),
            pl.BlockSpec(memory_space=pl.ANY),
            pl.BlockSpec((D_MODEL, tf), lambda i, f: (0, f)),
            pl.BlockSpec((tf, D_MODEL), lambda i, f: (f, 0)),
            pl.BlockSpec((1, D_MODEL), lambda i, f: (0, 0)),
        ],
        out_specs=pl.BlockSpec((tm, D_MODEL), lambda i, f: (i, 0)),
        out_shape=jax.ShapeDtypeStruct((m, D_MODEL), _F32),
        scratch_shapes=[pltpu.VMEM((tm, D_MODEL), _F32), pltpu.SemaphoreType.DMA(())],
        compiler_params=_params("arbitrary", "arbitrary"),
        name="mlp",
    )(h2, x1, w_up, w_down, g)


def kernel(x_prompt, x_sample, state_gla, state_pool, norm_mix_g, w_in, w_alpha_up, b_alpha,
           gla_norm_g, pool_w, pool_scale, w_out, norm_mlp_g, w_up, w_down, norm_final_g):
    batch, seq, _ = x_prompt.shape
    dec_batch, dec_seq, _ = x_sample.shape
    n_p = batch * seq
    n_s = dec_batch * dec_seq
    xp = x_prompt.reshape(n_p, D_MODEL)
    xs = x_sample.reshape(n_s, D_MODEL)
    row = lambda v: v.reshape(1, -1).astype(_F32)

    wup_pad = jnp.pad(w_alpha_up, ((0, LANES - GATE_RANK), (0, 0))).astype(_BF16)
    wup3 = jnp.concatenate([wup_pad] * 3, axis=0)
    wt = w_in.T

    h, la, u, w_out_b, pool_w_b = _norm_gate_u(xp, xs, row(norm_mix_g), wt, wup3, row(b_alpha),
                                               w_out, pool_w, tm=512)
    z, w_up_b, w_down_b = _proj(
        h, wt, ((0, QKVR_W // PROJ_TN), (GA_OFF, 2 * D_MODEL // PROJ_TN)), _BF16, 1024,
        ((w_up, (512, 1024)), (w_down, (256, D_MODEL))))

    oa_p, s_gla_p, oa_s, s_gla_s = _gla(z, la, state_gla, row(gla_norm_g), batch, seq, dec_seq,
                                        c=64, rows_per_step=256)
    buf_p = jnp.stack([u[(b + 1) * seq - POOL_BUF:(b + 1) * seq] for b in range(batch)], axis=0)

    merge_w = (pool_w_b, row(pool_scale), w_out_b, row(norm_mlp_g))
    x1_p, h2_p = _merge_prompt(xp, oa_p, u, z, *merge_w, batch, seq, tt=512)
    x1_s, h2_s, buf_s = _merge_sample(xs, oa_s, u, state_pool, z, n_p, *merge_w, dec_seq, bb=32)
    mlp_w = (w_up_b, w_down_b, row(norm_final_g))
    y_p = _mlp(h2_p, x1_p, *mlp_w, tm=1024, tf=512).reshape(batch, seq, D_MODEL)
    y_s = _mlp(h2_s, x1_s, *mlp_w, tm=1024, tf=512).reshape(dec_batch, dec_seq, D_MODEL)
    return (y_p, y_s, s_gla_p, buf_p, s_gla_s, buf_s)
```

```python
import functools

import jax
import jax.numpy as jnp
import numpy as np
from jax import lax
from jax.experimental import pallas as pl
from jax.experimental.pallas import tpu as pltpu

D_MODEL = 2048
GLA_HEADS = 4
GLA_DK = 256
GLA_DV = 512
GATE_RANK = 16
GATE_TAU = 16.0
POOL_WIDTH = 1024
POOL_GROUPS = 4
POOL_GROUP_W = 256
POOL_OUT_GROUP_W = 512
POOL_WINDOWS = (2, 4, 8, 16)
POOL_BUF = 15
D_FF = 8192
EPS = 1e-6
PAST_LEN = 16384
LOG2_E = 1.4426950408889634

Q_W = GLA_HEADS * GLA_DK
V_W = GLA_HEADS * GLA_DV
QKVR_W = 2 * Q_W + 2 * V_W
A_OFF = QKVR_W
U_OFF = A_OFF + GATE_RANK
GA_OFF = U_OFF + POOL_WIDTH
Z_Q, Z_K, Z_V, Z_R = 0, Q_W, 2 * Q_W, 2 * Q_W + V_W
Z_GA, Z_GB = QKVR_W, QKVR_W + D_MODEL

LANES = 128
PROJ_TN = 1024
HALO = 16
VMEM_LIMIT = 56 * 1024 * 1024

_F32 = jnp.float32
_BF16 = jnp.bfloat16


def _dot(a, b):
    return jnp.dot(a, b, preferred_element_type=_F32)


def _dot_nt(a, b):
    return lax.dot_general(a, b, (((1,), (1,)), ((), ())), preferred_element_type=_F32)


def _dot_tn(a, b):
    return lax.dot_general(a, b, (((0,), (0,)), ((), ())), preferred_element_type=_F32)


def _split3(x):
    hi = x.astype(_BF16)
    r1 = x - hi.astype(_F32)
    mid = r1.astype(_BF16)
    lo = (r1 - mid.astype(_F32)).astype(_BF16)
    return hi, mid, lo


def _rms_scale(x):
    return lax.rsqrt(jnp.mean(x * x, axis=-1, keepdims=True) + EPS)


def _params(*sem):
    return pltpu.CompilerParams(dimension_semantics=sem, vmem_limit_bytes=VMEM_LIMIT)


def _norm_gate_u_kernel(xp_ref, xs_ref, g_ref, wa_ref, wup_ref, ba_ref, wu_ref, wo_ref, pw_ref,
                        h_ref, la_ref, u_ref, wo_out_ref, pw_out_ref, wu_scr,
                        *, prompt_tiles, sub, n_side):
    i = pl.program_id(0)
    rc = 256

    @pl.when(i == 0)
    def _():
        for r in range(0, POOL_WIDTH, rc):
            wu_scr[r:r + rc, :] = wu_ref[r:r + rc, :].astype(_BF16)
        pw_out_ref[...] = pw_ref[...].astype(_BF16)

    @pl.when(i < n_side)
    def _():
        wo_out_ref[...] = wo_ref[...].astype(_BF16)

    def compute(x_ref):
        wa = wa_ref[...].astype(_BF16)
        for r0 in range(0, x_ref.shape[0], sub):
            rows = slice(r0, r0 + sub)
            x = x_ref[rows, :]
            hb = (x * _rms_scale(x) * g_ref[...]).astype(_BF16)
            h_ref[rows, :] = hb
            u_ref[rows, :] = _dot_nt(hb, wu_scr[...])
            a_lr = _dot_nt(hb, wa)
            a_hi, a_mid, a_lo = _split3(a_lr)
            a3 = jnp.concatenate([a_hi, a_mid, a_lo], axis=1)
            a_logit = _dot(a3, wup_ref[...]) + ba_ref[...]
            ls = jnp.minimum(a_logit, 0.0) - jnp.log1p(jnp.exp(-jnp.abs(a_logit)))
            la_ref[rows, :] = ls * (LOG2_E / GATE_TAU)

    @pl.when(i < prompt_tiles)
    def _():
        compute(xp_ref)

    @pl.when(i >= prompt_tiles)
    def _():
        compute(xs_ref)


def _norm_gate_u(xp, xs, g, wt, wup3, b_alpha, w_out, pool_w, tm):
    n_p, n_s = xp.shape[0], xs.shape[0]
    pt, st = n_p // tm, n_s // tm
    m = n_p + n_s
    const = lambda i: (0, 0)
    wo_rows = 128
    n_side = w_out.shape[0] // wo_rows
    assert n_side <= pt + st and n_side * wo_rows == w_out.shape[0]
    wo_spec = pl.BlockSpec((wo_rows, D_MODEL), lambda i: (jnp.minimum(i, n_side - 1), 0))
    pw_spec = pl.BlockSpec(pool_w.shape, lambda i: (0, 0, 0))
    return pl.pallas_call(
        functools.partial(_norm_gate_u_kernel, prompt_tiles=pt, sub=min(tm, 256), n_side=n_side),
        grid=(pt + st,),
        in_specs=[
            pl.BlockSpec((tm, D_MODEL), lambda i: (jnp.minimum(i, pt - 1), 0)),
            pl.BlockSpec((tm, D_MODEL), lambda i: (jnp.maximum(i - pt, 0), 0)),
            pl.BlockSpec((1, D_MODEL), const),
            pl.BlockSpec((LANES, D_MODEL), lambda i: (A_OFF // LANES, 0)),
            pl.BlockSpec((3 * LANES, Q_W), const),
            pl.BlockSpec((1, Q_W), const),
            pl.BlockSpec((pl.Element(POOL_WIDTH), pl.Element(D_MODEL)), lambda i: (U_OFF, 0),
                         pipeline_mode=pl.Buffered(1)),
            wo_spec,
            pw_spec,
        ],
        out_specs=[
            pl.BlockSpec((tm, D_MODEL), lambda i: (i, 0)),
            pl.BlockSpec((tm, Q_W), lambda i: (i, 0)),
            pl.BlockSpec((tm, POOL_WIDTH), lambda i: (i, 0)),
            wo_spec,
            pw_spec,
        ],
        out_shape=[
            jax.ShapeDtypeStruct((m, D_MODEL), _BF16),
            jax.ShapeDtypeStruct((m, Q_W), _F32),
            jax.ShapeDtypeStruct((m, POOL_WIDTH), _F32),
            jax.ShapeDtypeStruct(w_out.shape, _BF16),
            jax.ShapeDtypeStruct(pool_w.shape, _BF16),
        ],
        scratch_shapes=[pltpu.VMEM((POOL_WIDTH, D_MODEL), _BF16)],
        compiler_params=_params("arbitrary"),
        name="norm_gate_u",
    )(xp, xs, g, wt, wup3, b_alpha, wt, w_out, pool_w)


def _proj_kernel(h_ref, wt_ref, *refs, side_steps):
    n_s = len(side_steps)
    side_refs, z_ref, side_out_refs, w_scr = refs[:n_s], refs[n_s], refs[n_s + 1:-1], refs[-1]
    rc = 256

    @pl.when(pl.program_id(1) == 0)
    def _():
        for r in range(0, w_scr.shape[0], rc):
            w_scr[r:r + rc, :] = wt_ref[r:r + rc, :].astype(_BF16)

    z_ref[...] = _dot_nt(h_ref[...], w_scr[...]).astype(z_ref.dtype)

    step = pl.program_id(0) * pl.num_programs(1) + pl.program_id(1)
    for (lo, hi), src, dst in zip(side_steps, side_refs, side_out_refs):
        @pl.when((step >= lo) & (step < hi))
        def _(src=src, dst=dst):
            dst[...] = src[...].astype(_BF16)


def _proj(h, wt, pieces, out_dtype, tm, sides, tn=PROJ_TN):
    m = h.shape[0]
    (col0, n0), (col1, n1) = pieces
    n_tiles = n0 + n1
    n_i = m // tm
    assert all(c % 8 == 0 and c + n * tn <= wt.shape[0] for c, n in pieces)

    side_specs, side_steps, first = [], [], 0
    for arr, blk in sides:
        sr, sc = arr.shape[0] // blk[0], arr.shape[1] // blk[1]
        assert sr * blk[0] == arr.shape[0] and sc * blk[1] == arr.shape[1]
        n_blk = sr * sc

        def side_idx(j, i, first=first, n_blk=n_blk, sc=sc):
            b = jnp.clip(j * n_i + i - first, 0, n_blk - 1)
            return (b // sc, b % sc)

        side_specs.append(pl.BlockSpec(blk, side_idx))
        side_steps.append((first, first + n_blk))
        first += n_blk
    assert first <= n_tiles * n_i

    def w_idx(j, i):
        start = jnp.where(j < n0, col0 + j * tn, col1 + (j - n0) * tn)
        return (pl.multiple_of(start, 8), 0)

    return pl.pallas_call(
        functools.partial(_proj_kernel, side_steps=tuple(side_steps)),
        grid=(n_tiles, n_i),
        in_specs=[
            pl.BlockSpec((tm, D_MODEL), lambda j, i: (i, 0)),
            pl.BlockSpec((pl.Element(tn), pl.Element(D_MODEL)), w_idx),
        ] + side_specs,
        out_specs=[pl.BlockSpec((tm, tn), lambda j, i: (i, j))] + side_specs,
        out_shape=[jax.ShapeDtypeStruct((m, n_tiles * tn), out_dtype)]
        + [jax.ShapeDtypeStruct(arr.shape, _BF16) for arr, _ in sides],
        scratch_shapes=[pltpu.VMEM((tn, D_MODEL), _BF16)],
        compiler_params=_params("arbitrary", "arbitrary"),
        name="proj",
    )(h, wt, *[arr for arr, _ in sides])


def _gla_consts(c, per=1):
    g = c // per
    levels = int(np.log2(g))
    assert 2 ** levels == g and g * per == c
    t = np.arange(c)
    seq_id = t // g
    mats = [np.tril(np.ones((c, c), np.float32)) * (seq_id[:, None] == seq_id[None, :])]
    masks = [np.eye(c, dtype=np.float32)]
    for l in range(levels):
        h = 2 ** l
        blk = t // (2 * h)
        mid = blk * 2 * h + h - 1
        upper = (t // h) % 2 == 1
        m = np.zeros((c, c), np.float32)
        for row in range(c):
            if upper[row]:
                m[row, mid[row] + 1: row + 1] = 1.0
            else:
                m[row, row + 1: mid[row] + 1] = 1.0
        mats.append(m)
        masks.append(((blk[:, None] == blk[None, :]) & upper[:, None] & (~upper)[None, :])
                     .astype(np.float32))
    m_all = np.concatenate(mats, axis=0)
    m_cat = np.concatenate([m_all, m_all, m_all], axis=1)
    rowmasks = np.stack([(seq_id == e).astype(np.float32)[:, None] for e in range(per)])
    ones_cols = np.stack([np.tile(np.broadcast_to(rowmasks[e], (c, LANES)), (3, 1))
                          for e in range(per)])
    return (jnp.asarray(m_cat, _BF16), jnp.asarray(np.stack(masks), _F32),
            jnp.asarray(ones_cols, _BF16), jnp.asarray(rowmasks, _F32), levels)


def _gla_post(o, r, g):
    o_n = o * _rms_scale(o) * g
    return o_n * (r * jax.nn.sigmoid(r))


def _gla_rows(q_ref, k_ref, v_ref, r_ref, la_ref, oa_ref, rows, load_state, store_state, consts,
              c, levels, g):
    m_cat, masks, ones_cols, rowmasks = consts
    heads = range(GLA_HEADS)
    per = len(ones_cols)
    gsz = c // per
    dk = lambda h: slice(h * GLA_DK, (h + 1) * GLA_DK)
    dv = lambda h: slice(h * GLA_DV, (h + 1) * GLA_DV)
    sl = lambda a, h: a[:, dk(h)]
    qs = [q_ref[rows, dk(h)].astype(_F32) * (GLA_DK ** -0.5) for h in heads]
    ks = [k_ref[rows, dk(h)].astype(_F32) for h in heads]
    vbs = [v_ref[rows, dv(h)] for h in heads]
    hi, mid, lo = _split3(la_ref[rows, :])
    la3 = jnp.concatenate([hi, mid, lo], axis=0)
    eb = _dot(m_cat, la3)
    dcols = [jnp.exp2(_dot_tn(la3, ones_cols[e])) for e in range(per)]
    yield
    b = eb[0:c]
    scores = [jnp.where(masks[0], _dot_nt(qs[h].astype(_BF16), ks[h].astype(_BF16)), 0.0)
              for h in heads]
    for l in range(levels):
        x = jnp.exp2(eb[(l + 1) * c:(l + 2) * c])
        for h in heads:
            xh = sl(x, h)
            s_l = _dot_nt((qs[h] * xh).astype(_BF16), (ks[h] * xh).astype(_BF16))
            scores[h] = jnp.where(masks[l + 1], s_l, scores[h])
        yield
    eb0 = jnp.exp2(b)
    ebl = jnp.exp2(jnp.concatenate(
        [b[(e + 1) * gsz - 1:(e + 1) * gsz, :] - b[e * gsz:(e + 1) * gsz, :] for e in range(per)],
        axis=0))
    pick = (lambda a, e: a) if per == 1 else (lambda a, e: a * rowmasks[e])
    states = [[load_state(h, e) for e in range(per)] for h in heads]
    outs = []
    for h in heads:
        o = _dot(scores[h].astype(_BF16), vbs[h])
        qd = qs[h] * sl(eb0, h)
        for e in range(per):
            o = o + _dot(pick(qd, e).astype(_BF16), states[h][e].astype(_BF16))
        outs.append(o)
    yield
    for h in heads:
        kd = ks[h] * sl(ebl, h)
        for e in range(per):
            dh = dcols[e][dk(h)]
            decay = jnp.concatenate([dh] * (GLA_DV // LANES), axis=1)
            store_state(h, e, decay * states[h][e] + _dot_tn(pick(kd, e).astype(_BF16), vbs[h]))
    yield
    for h in heads:
        oa_ref[rows, dv(h)] = _gla_post(outs[h], r_ref[rows, dv(h)].astype(_F32), g
                                        ).astype(oa_ref.dtype)


def _run_staggered(tasks):
    live = list(tasks)
    t = 0
    while live:
        for item in list(live):
            start, gen = item
            if start <= t:
                try:
                    next(gen)
                except StopIteration:
                    live.remove(item)
        t += 1


def _load_gla_consts(mc_ref, mk_ref, oc_ref, rm_ref):
    return (mc_ref[...], [mk_ref[i] > 0 for i in range(mk_ref.shape[0])],
            [oc_ref[e] for e in range(oc_ref.shape[0])], [rm_ref[e] for e in range(rm_ref.shape[0])])


def _const_specs(arrays):
    return [pl.BlockSpec(a.shape, lambda *idx, nd=a.ndim: (0,) * nd) for a in arrays]


def _gla_kernel(qp_ref, kp_ref, vp_ref, rp_ref, lap_ref, qs_ref, ks_ref, vs_ref, rs_ref, las_ref,
                st_ref, g_ref, mcp_ref, mkp_ref, ocp_ref, rmp_ref, mcs_ref, mks_ref, ocs_ref, rms_ref,
                oap_ref, soutp_ref, oas_ref, souts_ref, s_scr, *, c, levels, n_sub, cs, per, levels_s,
                n_slabs):
    n = pl.program_id(1)

    @pl.when(n == 0)
    def _():
        s_scr[...] = jnp.zeros_like(s_scr)

    consts_p = _load_gla_consts(mcp_ref, mkp_ref, ocp_ref, rmp_ref)
    consts_s = _load_gla_consts(mcs_ref, mks_ref, ocs_ref, rms_ref)
    g = g_ref[...]

    def prompt_load(h, e):
        return s_scr[h]

    def prompt_store(h, e, value):
        s_scr[h] = value

    every = n_sub // n_slabs
    for i0 in range(0, n_sub, every):
        j = i0 // every

        def slab_load(h, e, j=j):
            return st_ref[j * per + e, h]

        def slab_store(h, e, value, j=j):
            souts_ref[j * per + e, h] = value

        tasks = [(2 * k, _gla_rows(qp_ref, kp_ref, vp_ref, rp_ref, lap_ref, oap_ref,
                                   slice((i0 + k) * c, (i0 + k + 1) * c), prompt_load, prompt_store,
                                   consts_p, c, levels, g)) for k in range(every)]
        tasks.insert(1, (1, _gla_rows(qs_ref, ks_ref, vs_ref, rs_ref, las_ref, oas_ref,
                                      slice(j * cs, (j + 1) * cs), slab_load, slab_store, consts_s,
                                      cs, levels_s, g)))
        _run_staggered(tasks)

    @pl.when(n == pl.num_programs(1) - 1)
    def _():
        soutp_ref[0] = s_scr[...]


def _gla(z, la, state, gla_g, batch, seq, dec_seq, c, rows_per_step):
    cs = 16
    per = cs // dec_seq
    dec_batch = state.shape[0]
    n_p = batch * seq
    steps = seq // rows_per_step
    n_sub = rows_per_step // c
    n_slabs = dec_batch // per // (batch * steps)
    assert cs % dec_seq == 0 and batch * steps * n_slabs * per == dec_batch and n_p % cs == 0
    assert n_sub % n_slabs == 0
    srows, sseqs = n_slabs * cs, n_slabs * per
    *consts_p, levels = _gla_consts(c)
    *consts_s, levels_s = _gla_consts(cs, per)
    kern = functools.partial(_gla_kernel, c=c, levels=levels, n_sub=n_sub, cs=cs, per=per,
                             levels_s=levels_s, n_slabs=n_slabs)
    rb = lambda b, n: b * steps + n
    sb = lambda b, n: n_p // srows + rb(b, n)
    return pl.pallas_call(
        kern,
        grid=(batch, steps),
        in_specs=[
            pl.BlockSpec((rows_per_step, Q_W), lambda b, n: (rb(b, n), Z_Q // Q_W)),
            pl.BlockSpec((rows_per_step, Q_W), lambda b, n: (rb(b, n), Z_K // Q_W)),
            pl.BlockSpec((rows_per_step, V_W), lambda b, n: (rb(b, n), Z_V // V_W)),
            pl.BlockSpec((rows_per_step, V_W), lambda b, n: (rb(b, n), Z_R // V_W)),
            pl.BlockSpec((rows_per_step, Q_W), lambda b, n: (rb(b, n), 0)),
            pl.BlockSpec((srows, Q_W), lambda b, n: (sb(b, n), Z_Q // Q_W)),
            pl.BlockSpec((srows, Q_W), lambda b, n: (sb(b, n), Z_K // Q_W)),
            pl.BlockSpec((srows, V_W), lambda b, n: (sb(b, n), Z_V // V_W)),
            pl.BlockSpec((srows, V_W), lambda b, n: (sb(b, n), Z_R // V_W)),
            pl.BlockSpec((srows, Q_W), lambda b, n: (sb(b, n), 0)),
            pl.BlockSpec((sseqs, GLA_HEADS, GLA_DK, GLA_DV), lambda b, n: (rb(b, n), 0, 0, 0)),
            pl.BlockSpec((1, GLA_DV), lambda b, n: (0, 0)),
        ] + _const_specs(consts_p) + _const_specs(consts_s),
        out_specs=[
            pl.BlockSpec((rows_per_step, V_W), lambda b, n: (rb(b, n), 0)),
            pl.BlockSpec((1, GLA_HEADS, GLA_DK, GLA_DV), lambda b, n: (b, 0, 0, 0)),
            pl.BlockSpec((srows, V_W), lambda b, n: (rb(b, n), 0)),
            pl.BlockSpec((sseqs, GLA_HEADS, GLA_DK, GLA_DV), lambda b, n: (rb(b, n), 0, 0, 0)),
        ],
        out_shape=[
            jax.ShapeDtypeStruct((n_p, V_W), _BF16),
            jax.ShapeDtypeStruct((batch, GLA_HEADS, GLA_DK, GLA_DV), _F32),
            jax.ShapeDtypeStruct((dec_batch * dec_seq, V_W), _BF16),
            jax.ShapeDtypeStruct((dec_batch, GLA_HEADS, GLA_DK, GLA_DV), _F32),
        ],
        scratch_shapes=[pltpu.VMEM((GLA_HEADS, GLA_DK, GLA_DV), _F32)],
        compiler_params=_params("arbitrary", "arbitrary"),
        name="gla",
    )(z, z, z, z, la, z, z, z, z, la, state, gla_g, *consts_p, *consts_s)


def _window_sums(ext_ref, t0, tt, lead):
    axis = len(lead)
    outs = []
    for gi, w in enumerate(POOL_WINDOWS):
        assert w & (w - 1) == 0 and w - 1 <= HALO
        lanes = slice(gi * POOL_GROUP_W, (gi + 1) * POOL_GROUP_W)
        s = ext_ref[lead + (slice(t0 - HALO, t0 + tt), lanes)]
        span = 1
        while span < w:
            s = s + pltpu.roll(s, span, axis=axis)
            span *= 2
        outs.append(s[lead + (slice(HALO, HALO + tt),)])
    return outs


def _pool_prompt_rows(ext_scr, pos0, r0, nr):
    sums = _window_sums(ext_scr, HALO + r0, nr, ())
    pos = (pos0 + r0 + 1 + lax.broadcasted_iota(jnp.int32, (nr, 1), 0)).astype(_F32)
    ps = []
    for gi, w in enumerate(POOL_WINDOWS):
        lanes = slice(gi * POOL_GROUP_W, (gi + 1) * POOL_GROUP_W)
        ps.append(sums[gi] / jnp.minimum(pos, float(w)) - ext_scr[HALO + r0:HALO + r0 + nr, lanes])
    return ps


def _pool_sample_tile(u_ref, buf_ref, nb_ref, ext_scr, bb, seq):
    u = u_ref[...].reshape(bb, seq, POOL_WIDTH)
    ext_scr[:, 0:HALO - POOL_BUF, :] = jnp.zeros((bb, HALO - POOL_BUF, POOL_WIDTH), _F32)
    ext_scr[:, HALO - POOL_BUF:HALO, :] = buf_ref[...]
    ext_scr[:, HALO:HALO + seq, :] = u
    sums = _window_sums(ext_scr, HALO, seq, (slice(None),))
    ps = []
    for gi, w in enumerate(POOL_WINDOWS):
        lanes = slice(gi * POOL_GROUP_W, (gi + 1) * POOL_GROUP_W)
        ps.append((sums[gi] / float(w) - u[:, :, lanes]).reshape(bb * seq, POOL_GROUP_W))
    nb_ref[...] = ext_scr[:, HALO + seq - POOL_BUF:HALO + seq, :]
    return ps


def _merge_gate(rows, oa_ref, ps, ga_ref, gb_ref, pw_ref, ps_ref):
    ob = jnp.concatenate([_dot(ps[gi].astype(_BF16), pw_ref[gi]) for gi in range(POOL_GROUPS)],
                         axis=1) * ps_ref[...]
    m = (jax.nn.sigmoid(ga_ref[rows, :].astype(_F32)) * oa_ref[rows, :].astype(_F32)
         + jax.nn.sigmoid(gb_ref[rows, :].astype(_F32)) * ob)
    return m.astype(_BF16)


def _merge_project(rows, x_ref, m, wo_ref, g2_ref, x1_ref, h2_ref):
    x1 = x_ref[rows, :] + _dot(m, wo_ref[...])
    x1_ref[rows, :] = x1
    h2_ref[rows, :] = (x1 * _rms_scale(x1) * g2_ref[...]).astype(_BF16)


def _merge_core(rows, x_ref, oa_ref, ps, ga_ref, gb_ref, pw_ref, ps_ref, wo_ref, g2_ref,
                x1_ref, h2_ref):
    m = _merge_gate(rows, oa_ref, ps, ga_ref, gb_ref, pw_ref, ps_ref)
    _merge_project(rows, x_ref, m, wo_ref, g2_ref, x1_ref, h2_ref)


def _merge_prompt_kernel(x_ref, oa_ref, u_ref, ga_ref, gb_ref, pw_ref, ps_ref, wo_ref, g2_ref,
                         x1_ref, h2_ref, ext_scr, *, tt, sub):
    n = pl.program_id(1)

    @pl.when(n == 0)
    def _():
        ext_scr[0:HALO, :] = jnp.zeros((HALO, POOL_WIDTH), _F32)

    ext_scr[HALO:HALO + tt, :] = u_ref[...]
    for r0 in range(0, tt, sub):
        ps = _pool_prompt_rows(ext_scr, n * tt, r0, sub)
        _merge_core(slice(r0, r0 + sub), x_ref, oa_ref, ps, ga_ref, gb_ref, pw_ref, ps_ref, wo_ref,
                    g2_ref, x1_ref, h2_ref)
    ext_scr[0:HALO, :] = ext_scr[tt:tt + HALO, :]


def _merge_sample_kernel(x_ref, oa_ref, u_ref, buf_ref, ga_ref, gb_ref, pw_ref, ps_ref, wo_ref,
                         g2_ref, x1_ref, h2_ref, nb_ref, ext_scr, *, bb, seq):
    ps = _pool_sample_tile(u_ref, buf_ref, nb_ref, ext_scr, bb, seq)
    _merge_core(slice(None), x_ref, oa_ref, ps, ga_ref, gb_ref, pw_ref, ps_ref, wo_ref, g2_ref,
                x1_ref, h2_ref)


def _weight_specs():
    zeros = lambda nd: (lambda *idx: (0,) * nd)
    once = pl.Buffered(1)
    return [
        pl.BlockSpec((POOL_GROUPS, POOL_GROUP_W, POOL_OUT_GROUP_W), zeros(3), pipeline_mode=once),
        pl.BlockSpec((1, D_MODEL), zeros(2)),
        pl.BlockSpec((D_MODEL, D_MODEL), zeros(2), pipeline_mode=once),
        pl.BlockSpec((1, D_MODEL), zeros(2)),
    ]


def _merge_prompt(x, oa, u, zg, pool_w, pool_scale, w_out, g2, batch, seq, tt):
    steps = seq // tt
    rb = lambda b, n: b * steps + n
    return pl.pallas_call(
        functools.partial(_merge_prompt_kernel, tt=tt, sub=256),
        grid=(batch, steps),
        in_specs=[
            pl.BlockSpec((tt, D_MODEL), lambda b, n: (rb(b, n), 0)),
            pl.BlockSpec((tt, D_MODEL), lambda b, n: (rb(b, n), 0)),
            pl.BlockSpec((tt, POOL_WIDTH), lambda b, n: (rb(b, n), 0)),
            pl.BlockSpec((tt, D_MODEL), lambda b, n: (rb(b, n), Z_GA // D_MODEL)),
            pl.BlockSpec((tt, D_MODEL), lambda b, n: (rb(b, n), Z_GB // D_MODEL)),
        ] + _weight_specs(),
        out_specs=[
            pl.BlockSpec((tt, D_MODEL), lambda b, n: (rb(b, n), 0)),
            pl.BlockSpec((tt, D_MODEL), lambda b, n: (rb(b, n), 0)),
        ],
        out_shape=[
            jax.ShapeDtypeStruct((batch * seq, D_MODEL), _F32),
            jax.ShapeDtypeStruct((batch * seq, D_MODEL), _BF16),
        ],
        scratch_shapes=[pltpu.VMEM((HALO + tt, POOL_WIDTH), _F32)],
        compiler_params=_params("arbitrary", "arbitrary"),
        name="merge_prompt",
    )(x, oa, u, zg, zg, pool_w, pool_scale, w_out, g2)


def _merge_sample(x, oa, u, buf, zg, row0, pool_w, pool_scale, w_out, g2, seq, bb):
    batch = buf.shape[0]
    rows = bb * seq
    rb0 = row0 // rows
    assert PAST_LEN >= max(POOL_WINDOWS) and row0 % rows == 0 and seq == 8
    return pl.pallas_call(
        functools.partial(_merge_sample_kernel, bb=bb, seq=seq),
        grid=(batch // bb,),
        in_specs=[
            pl.BlockSpec((rows, D_MODEL), lambda i: (i, 0)),
            pl.BlockSpec((rows, D_MODEL), lambda i: (i, 0)),
            pl.BlockSpec((rows, POOL_WIDTH), lambda i: (rb0 + i, 0)),
            pl.BlockSpec((bb, POOL_BUF, POOL_WIDTH), lambda i: (i, 0, 0)),
            pl.BlockSpec((rows, D_MODEL), lambda i: (rb0 + i, Z_GA // D_MODEL)),
            pl.BlockSpec((rows, D_MODEL), lambda i: (rb0 + i, Z_GB // D_MODEL)),
        ] + _weight_specs(),
        out_specs=[
            pl.BlockSpec((rows, D_MODEL), lambda i: (i, 0)),
            pl.BlockSpec((rows, D_MODEL), lambda i: (i, 0)),
            pl.BlockSpec((bb, POOL_BUF, POOL_WIDTH), lambda i: (i, 0, 0)),
        ],
        out_shape=[
            jax.ShapeDtypeStruct((batch * seq, D_MODEL), _F32),
            jax.ShapeDtypeStruct((batch * seq, D_MODEL), _BF16),
            jax.ShapeDtypeStruct((batch, POOL_BUF, POOL_WIDTH), _F32),
        ],
        scratch_shapes=[pltpu.VMEM((bb, HALO + seq, POOL_WIDTH), _F32)],
        compiler_params=_params("arbitrary"),
        name="merge_sample",
    )(x, oa, u, buf, zg, zg, pool_w, pool_scale, w_out, g2)


def _mlp_kernel(h2_ref, x1_hbm, wu_ref, wd_ref, g_ref, y_ref, x1_scr, x1_sem):
    i = pl.program_id(0)
    f = pl.program_id(1)
    tm = y_ref.shape[0]

    def x1_copy():
        return pltpu.make_async_copy(x1_hbm.at[pl.ds(pl.multiple_of(i * tm, tm), tm), :],
                                     x1_scr, x1_sem)

    @pl.when(f == 0)
    def _():
        x1_copy().start()
        y_ref[...] = jnp.zeros_like(y_ref)

    a = jnp.maximum(_dot(h2_ref[...], wu_ref[...]), 0.0)
    y_ref[...] += _dot((a * a).astype(_BF16), wd_ref[...])

    @pl.when(f == pl.num_programs(1) - 1)
    def _():
        x1_copy().wait()
        x2 = y_ref[...] + x1_scr[...]
        y_ref[...] = x2 * _rms_scale(x2) * g_ref[...]


def _mlp(h2, x1, w_up, w_down, g, tm, tf):
    m = h2.shape[0]
    assert m % tm == 0
    return pl.pallas_call(
        _mlp_kernel,
        grid=(m // tm, D_FF // tf),
        in_specs=[
            pl.BlockSpec((tm, D_MODEL), lambda i, f: (i, 0)),
            pl.BlockSpec(memory_space=pl.ANY),
            pl.BlockSpec((D_MODEL, tf), lambda i, f: (0, f)),
            pl.BlockSpec((tf, D_MODEL), lambda i, f: (f, 0)),
            pl.BlockSpec((1, D_MODEL), lambda i, f: (0, 0)),
        ],
        out_specs=pl.BlockSpec((tm, D_MODEL), lambda i, f: (i, 0)),
        out_shape=jax.ShapeDtypeStruct((m, D_MODEL), _F32),
        scratch_shapes=[pltpu.VMEM((tm, D_MODEL), _F32), pltpu.SemaphoreType.DMA(())],
        compiler_params=_params("arbitrary", "arbitrary"),
        name="mlp",
    )(h2, x1, w_up, w_down, g)


def kernel(x_prompt, x_sample, state_gla, state_pool, norm_mix_g, w_in, w_alpha_up, b_alpha,
           gla_norm_g, pool_w, pool_scale, w_out, norm_mlp_g, w_up, w_down, norm_final_g):
    batch, seq, _ = x_prompt.shape
    dec_batch, dec_seq, _ = x_sample.shape
    n_p = batch * seq
    n_s = dec_batch * dec_seq
    xp = x_prompt.reshape(n_p, D_MODEL)
    xs = x_sample.reshape(n_s, D_MODEL)
    row = lambda v: v.reshape(1, -1).astype(_F32)

    wup_pad = jnp.pad(w_alpha_up, ((0, LANES - GATE_RANK), (0, 0))).astype(_BF16)
    wup3 = jnp.concatenate([wup_pad] * 3, axis=0)
    wt = w_in.T

    h, la, u, w_out_b, pool_w_b = _norm_gate_u(xp, xs, row(norm_mix_g), wt, wup3, row(b_alpha),
                                               w_out, pool_w, tm=512)
    z, w_up_b, w_down_b = _proj(
        h, wt, ((0, QKVR_W // PROJ_TN), (GA_OFF, 2 * D_MODEL // PROJ_TN)), _BF16, 1024,
        ((w_up, (512, 1024)), (w_down, (256, D_MODEL))))

    oa_p, s_gla_p, oa_s, s_gla_s = _gla(z, la, state_gla, row(gla_norm_g), batch, seq, dec_seq,
                                        c=64, rows_per_step=128)
    buf_p = jnp.stack([u[(b + 1) * seq - POOL_BUF:(b + 1) * seq] for b in range(batch)], axis=0)

    merge_w = (pool_w_b, row(pool_scale), w_out_b, row(norm_mlp_g))
    x1_p, h2_p = _merge_prompt(xp, oa_p, u, z, *merge_w, batch, seq, tt=512)
    x1_s, h2_s, buf_s = _merge_sample(xs, oa_s, u, state_pool, z, n_p, *merge_w, dec_seq, bb=32)
    mlp_w = (w_up_b, w_down_b, row(norm_final_g))
    y_p = _mlp(h2_p, x1_p, *mlp_w, tm=1024, tf=512).reshape(batch, seq, D_MODEL)
    y_s = _mlp(h2_s, x1_s, *mlp_w, tm=1024, tf=512).reshape(dec_batch, dec_seq, D_MODEL)
    return (y_p, y_s, s_gla_p, buf_p, s_gla_s, buf_s)
```

```python
import functools

import jax
import jax.numpy as jnp
import numpy as np
from jax import lax
from jax.experimental import pallas as pl
from jax.experimental.pallas import tpu as pltpu

D_MODEL = 2048
GLA_HEADS = 4
GLA_DK = 256
GLA_DV = 512
GATE_RANK = 16
GATE_TAU = 16.0
POOL_WIDTH = 1024
POOL_GROUPS = 4
POOL_GROUP_W = 256
POOL_OUT_GROUP_W = 512
POOL_WINDOWS = (2, 4, 8, 16)
POOL_BUF = 15
D_FF = 8192
EPS = 1e-6
PAST_LEN = 16384
LOG2_E = 1.4426950408889634

Q_W = GLA_HEADS * GLA_DK
V_W = GLA_HEADS * GLA_DV
QKVR_W = 2 * Q_W + 2 * V_W
A_OFF = QKVR_W
U_OFF = A_OFF + GATE_RANK
GA_OFF = U_OFF + POOL_WIDTH
Z_Q, Z_K, Z_V, Z_R = 0, Q_W, 2 * Q_W, 2 * Q_W + V_W
Z_GA, Z_GB = QKVR_W, QKVR_W + D_MODEL

LANES = 128
PROJ_TN = 1024
HALO = 16
VMEM_LIMIT = 56 * 1024 * 1024

_F32 = jnp.float32
_BF16 = jnp.bfloat16


def _dot(a, b):
    return jnp.dot(a, b, preferred_element_type=_F32)


def _dot_nt(a, b):
    return lax.dot_general(a, b, (((1,), (1,)), ((), ())), preferred_element_type=_F32)


def _dot_tn(a, b):
    return lax.dot_general(a, b, (((0,), (0,)), ((), ())), preferred_element_type=_F32)


def _split3(x):
    hi = x.astype(_BF16)
    r1 = x - hi.astype(_F32)
    mid = r1.astype(_BF16)
    lo = (r1 - mid.astype(_F32)).astype(_BF16)
    return hi, mid, lo


def _rms_scale(x):
    return lax.rsqrt(jnp.mean(x * x, axis=-1, keepdims=True) + EPS)


def _params(*sem):
    return pltpu.CompilerParams(dimension_semantics=sem, vmem_limit_bytes=VMEM_LIMIT)


def _norm_gate_u_kernel(xp_ref, xs_ref, g_ref, wa_ref, wup_ref, ba_ref, wu_ref, wo_ref, pw_ref,
                        h_ref, la_ref, u_ref, wo_out_ref, pw_out_ref, wu_scr,
                        *, prompt_tiles, sub, n_side):
    i = pl.program_id(0)
    rc = 256

    @pl.when(i == 0)
    def _():
        for r in range(0, POOL_WIDTH, rc):
            wu_scr[r:r + rc, :] = wu_ref[r:r + rc, :].astype(_BF16)
        pw_out_ref[...] = pw_ref[...].astype(_BF16)

    @pl.when(i < n_side)
    def _():
        wo_out_ref[...] = wo_ref[...].astype(_BF16)

    def compute(x_ref):
        wa = wa_ref[...].astype(_BF16)
        for r0 in range(0, x_ref.shape[0], sub):
            rows = slice(r0, r0 + sub)
            x = x_ref[rows, :]
            hb = (x * _rms_scale(x) * g_ref[...]).astype(_BF16)
            h_ref[rows, :] = hb
            u_ref[rows, :] = _dot_nt(hb, wu_scr[...])
            a_lr = _dot_nt(hb, wa)
            a_hi, a_mid, a_lo = _split3(a_lr)
            a3 = jnp.concatenate([a_hi, a_mid, a_lo], axis=1)
            a_logit = _dot(a3, wup_ref[...]) + ba_ref[...]
            ls = jnp.minimum(a_logit, 0.0) - jnp.log1p(jnp.exp(-jnp.abs(a_logit)))
            la_ref[rows, :] = ls * (LOG2_E / GATE_TAU)

    @pl.when(i < prompt_tiles)
    def _():
        compute(xp_ref)

    @pl.when(i >= prompt_tiles)
    def _():
        compute(xs_ref)


def _norm_gate_u(xp, xs, g, wt, wup3, b_alpha, w_out, pool_w, tm):
    n_p, n_s = xp.shape[0], xs.shape[0]
    pt, st = n_p // tm, n_s // tm
    m = n_p + n_s
    const = lambda i: (0, 0)
    wo_rows = 128
    n_side = w_out.shape[0] // wo_rows
    assert n_side <= pt + st and n_side * wo_rows == w_out.shape[0]
    wo_spec = pl.BlockSpec((wo_rows, D_MODEL), lambda i: (jnp.minimum(i, n_side - 1), 0))
    pw_spec = pl.BlockSpec(pool_w.shape, lambda i: (0, 0, 0))
    return pl.pallas_call(
        functools.partial(_norm_gate_u_kernel, prompt_tiles=pt, sub=min(tm, 256), n_side=n_side),
        grid=(pt + st,),
        in_specs=[
            pl.BlockSpec((tm, D_MODEL), lambda i: (jnp.minimum(i, pt - 1), 0)),
            pl.BlockSpec((tm, D_MODEL), lambda i: (jnp.maximum(i - pt, 0), 0)),
            pl.BlockSpec((1, D_MODEL), const),
            pl.BlockSpec((LANES, D_MODEL), lambda i: (A_OFF // LANES, 0)),
            pl.BlockSpec((3 * LANES, Q_W), const),
            pl.BlockSpec((1, Q_W), const),
            pl.BlockSpec((pl.Element(POOL_WIDTH), pl.Element(D_MODEL)), lambda i: (U_OFF, 0),
                         pipeline_mode=pl.Buffered(1)),
            wo_spec,
            pw_spec,
        ],
        out_specs=[
            pl.BlockSpec((tm, D_MODEL), lambda i: (i, 0)),
            pl.BlockSpec((tm, Q_W), lambda i: (i, 0)),
            pl.BlockSpec((tm, POOL_WIDTH), lambda i: (i, 0)),
            wo_spec,
            pw_spec,
        ],
        out_shape=[
            jax.ShapeDtypeStruct((m, D_MODEL), _BF16),
            jax.ShapeDtypeStruct((m, Q_W), _F32),
            jax.ShapeDtypeStruct((m, POOL_WIDTH), _F32),
            jax.ShapeDtypeStruct(w_out.shape, _BF16),
            jax.ShapeDtypeStruct(pool_w.shape, _BF16),
        ],
        scratch_shapes=[pltpu.VMEM((POOL_WIDTH, D_MODEL), _BF16)],
        compiler_params=_params("arbitrary"),
        name="norm_gate_u",
    )(xp, xs, g, wt, wup3, b_alpha, wt, w_out, pool_w)


def _proj_kernel(h_ref, wt_ref, *refs, side_steps):
    n_s = len(side_steps)
    side_refs, z_ref, side_out_refs, w_scr = refs[:n_s], refs[n_s], refs[n_s + 1:-1], refs[-1]
    rc = 256

    @pl.when(pl.program_id(1) == 0)
    def _():
        for r in range(0, w_scr.shape[0], rc):
            w_scr[r:r + rc, :] = wt_ref[r:r + rc, :].astype(_BF16)

    z_ref[...] = _dot_nt(h_ref[...], w_scr[...]).astype(z_ref.dtype)

    step = pl.program_id(0) * pl.num_programs(1) + pl.program_id(1)
    for (lo, hi), src, dst in zip(side_steps, side_refs, side_out_refs):
        @pl.when((step >= lo) & (step < hi))
        def _(src=src, dst=dst):
            dst[...] = src[...].astype(_BF16)


def _proj(h, wt, pieces, out_dtype, tm, sides, tn=PROJ_TN):
    m = h.shape[0]
    (col0, n0), (col1, n1) = pieces
    n_tiles = n0 + n1
    n_i = m // tm
    assert all(c % 8 == 0 and c + n * tn <= wt.shape[0] for c, n in pieces)

    side_specs, side_steps, first = [], [], 0
    for arr, blk in sides:
        sr, sc = arr.shape[0] // blk[0], arr.shape[1] // blk[1]
        assert sr * blk[0] == arr.shape[0] and sc * blk[1] == arr.shape[1]
        n_blk = sr * sc

        def side_idx(j, i, first=first, n_blk=n_blk, sc=sc):
            b = jnp.clip(j * n_i + i - first, 0, n_blk - 1)
            return (b // sc, b % sc)

        side_specs.append(pl.BlockSpec(blk, side_idx))
        side_steps.append((first, first + n_blk))
        first += n_blk
    assert first <= n_tiles * n_i

    def w_idx(j, i):
        start = jnp.where(j < n0, col0 + j * tn, col1 + (j - n0) * tn)
        return (pl.multiple_of(start, 8), 0)

    return pl.pallas_call(
        functools.partial(_proj_kernel, side_steps=tuple(side_steps)),
        grid=(n_tiles, n_i),
        in_specs=[
            pl.BlockSpec((tm, D_MODEL), lambda j, i: (i, 0)),
            pl.BlockSpec((pl.Element(tn), pl.Element(D_MODEL)), w_idx),
        ] + side_specs,
        out_specs=[pl.BlockSpec((tm, tn), lambda j, i: (i, j))] + side_specs,
        out_shape=[jax.ShapeDtypeStruct((m, n_tiles * tn), out_dtype)]
        + [jax.ShapeDtypeStruct(arr.shape, _BF16) for arr, _ in sides],
        scratch_shapes=[pltpu.VMEM((tn, D_MODEL), _BF16)],
        compiler_params=_params("arbitrary", "arbitrary"),
        name="proj",
    )(h, wt, *[arr for arr, _ in sides])


def _gla_consts(c, per=1):
    g = c // per
    levels = int(np.log2(g))
    assert 2 ** levels == g and g * per == c
    t = np.arange(c)
    seq_id = t // g
    mats = [np.tril(np.ones((c, c), np.float32)) * (seq_id[:, None] == seq_id[None, :])]
    masks = [np.eye(c, dtype=np.float32)]
    for l in range(levels):
        h = 2 ** l
        blk = t // (2 * h)
        mid = blk * 2 * h + h - 1
        upper = (t // h) % 2 == 1
        m = np.zeros((c, c), np.float32)
        for row in range(c):
            if upper[row]:
                m[row, mid[row] + 1: row + 1] = 1.0
            else:
                m[row, row + 1: mid[row] + 1] = 1.0
        mats.append(m)
        masks.append(((blk[:, None] == blk[None, :]) & upper[:, None] & (~upper)[None, :])
                     .astype(np.float32))
    m_all = np.concatenate(mats, axis=0)
    m_cat = np.concatenate([m_all, m_all, m_all], axis=1)
    rowmasks = np.stack([(seq_id == e).astype(np.float32)[:, None] for e in range(per)])
    ones_cols = np.stack([np.tile(np.broadcast_to(rowmasks[e], (c, LANES)), (3, 1))
                          for e in range(per)])
    return (jnp.asarray(m_cat, _BF16), jnp.asarray(np.stack(masks), _F32),
            jnp.asarray(ones_cols, _BF16), jnp.asarray(rowmasks, _F32), levels)


def _gla_post(o, r, g):
    o_n = o * _rms_scale(o) * g
    return o_n * (r * jax.nn.sigmoid(r))


def _gla_rows(q_ref, k_ref, v_ref, r_ref, la_ref, oa_ref, rows, load_state, store_state, consts,
              c, levels, g):
    m_cat, masks, ones_cols, rowmasks = consts
    heads = range(GLA_HEADS)
    per = len(ones_cols)
    gsz = c // per
    dk = lambda h: slice(h * GLA_DK, (h + 1) * GLA_DK)
    dv = lambda h: slice(h * GLA_DV, (h + 1) * GLA_DV)
    sl = lambda a, h: a[:, dk(h)]
    qs = [q_ref[rows, dk(h)].astype(_F32) * (GLA_DK ** -0.5) for h in heads]
    ks = [k_ref[rows, dk(h)].astype(_F32) for h in heads]
    vbs = [v_ref[rows, dv(h)] for h in heads]
    hi, mid, lo = _split3(la_ref[rows, :])
    la3 = jnp.concatenate([hi, mid, lo], axis=0)
    eb = _dot(m_cat, la3)
    dcols = [jnp.exp2(_dot_tn(la3, ones_cols[e])) for e in range(per)]
    yield
    b = eb[0:c]
    scores = [jnp.where(masks[0], _dot_nt(qs[h].astype(_BF16), ks[h].astype(_BF16)), 0.0)
              for h in heads]
    for l in range(levels):
        x = jnp.exp2(eb[(l + 1) * c:(l + 2) * c])
        for h in heads:
            xh = sl(x, h)
            s_l = _dot_nt((qs[h] * xh).astype(_BF16), (ks[h] * xh).astype(_BF16))
            scores[h] = jnp.where(masks[l + 1], s_l, scores[h])
        yield
    eb0 = jnp.exp2(b)
    ebl = jnp.exp2(jnp.concatenate(
        [b[(e + 1) * gsz - 1:(e + 1) * gsz, :] - b[e * gsz:(e + 1) * gsz, :] for e in range(per)],
        axis=0))
    pick = (lambda a, e: a) if per == 1 else (lambda a, e: a * rowmasks[e])
    states = [[load_state(h, e) for e in range(per)] for h in heads]
    outs = []
    for h in heads:
        o = _dot(scores[h].astype(_BF16), vbs[h])
        qd = qs[h] * sl(eb0, h)
        for e in range(per):
            o = o + _dot(pick(qd, e).astype(_BF16), states[h][e].astype(_BF16))
        outs.append(o)
    yield
    for h in heads:
        kd = ks[h] * sl(ebl, h)
        for e in range(per):
            dh = dcols[e][dk(h)]
            decay = jnp.concatenate([dh] * (GLA_DV // LANES), axis=1)
            store_state(h, e, decay * states[h][e] + _dot_tn(pick(kd, e).astype(_BF16), vbs[h]))
    yield
    for h in heads:
        oa_ref[rows, dv(h)] = _gla_post(outs[h], r_ref[rows, dv(h)].astype(_F32), g
                                        ).astype(oa_ref.dtype)


def _run_staggered(tasks):
    live = list(tasks)
    t = 0
    while live:
        for item in list(live):
            start, gen = item
            if start <= t:
                try:
                    next(gen)
                except StopIteration:
                    live.remove(item)
        t += 1


def _load_gla_consts(mc_ref, mk_ref, oc_ref, rm_ref):
    return (mc_ref[...], [mk_ref[i] > 0 for i in range(mk_ref.shape[0])],
            [oc_ref[e] for e in range(oc_ref.shape[0])], [rm_ref[e] for e in range(rm_ref.shape[0])])


def _const_specs(arrays):
    return [pl.BlockSpec(a.shape, lambda *idx, nd=a.ndim: (0,) * nd) for a in arrays]


def _gla_kernel(qp_ref, kp_ref, vp_ref, rp_ref, lap_ref, qs_ref, ks_ref, vs_ref, rs_ref, las_ref,
                st_ref, g_ref, mcp_ref, mkp_ref, ocp_ref, rmp_ref, mcs_ref, mks_ref, ocs_ref, rms_ref,
                oap_ref, soutp_ref, oas_ref, souts_ref, s_scr, *, c, levels, n_sub, cs, per, levels_s,
                n_slabs):
    n = pl.program_id(1)

    @pl.when(n == 0)
    def _():
        s_scr[...] = jnp.zeros_like(s_scr)

    consts_p = _load_gla_consts(mcp_ref, mkp_ref, ocp_ref, rmp_ref)
    consts_s = _load_gla_consts(mcs_ref, mks_ref, ocs_ref, rms_ref)
    g = g_ref[...]

    def prompt_load(h, e):
        return s_scr[h]

    def prompt_store(h, e, value):
        s_scr[h] = value

    every = n_sub // n_slabs
    for i0 in range(0, n_sub, every):
        j = i0 // every

        def slab_load(h, e, j=j):
            return st_ref[j * per + e, h]

        def slab_store(h, e, value, j=j):
            souts_ref[j * per + e, h] = value

        tasks = [(2 * k, _gla_rows(qp_ref, kp_ref, vp_ref, rp_ref, lap_ref, oap_ref,
                                   slice((i0 + k) * c, (i0 + k + 1) * c), prompt_load, prompt_store,
                                   consts_p, c, levels, g)) for k in range(every)]
        tasks.insert(1, (1, _gla_rows(qs_ref, ks_ref, vs_ref, rs_ref, las_ref, oas_ref,
                                      slice(j * cs, (j + 1) * cs), slab_load, slab_store, consts_s,
                                      cs, levels_s, g)))
        _run_staggered(tasks)

    @pl.when(n == pl.num_programs(1) - 1)
    def _():
        soutp_ref[0] = s_scr[...]


def _gla(z, la, state, gla_g, batch, seq, dec_seq, c, rows_per_step):
    cs = 16
    per = cs // dec_seq
    dec_batch = state.shape[0]
    n_p = batch * seq
    steps = seq // rows_per_step
    n_sub = rows_per_step // c
    n_slabs = dec_batch // per // (batch * steps)
    assert cs % dec_seq == 0 and batch * steps * n_slabs * per == dec_batch and n_p % cs == 0
    assert n_sub % n_slabs == 0
    srows, sseqs = n_slabs * cs, n_slabs * per
    *consts_p, levels = _gla_consts(c)
    *consts_s, levels_s = _gla_consts(cs, per)
    kern = functools.partial(_gla_kernel, c=c, levels=levels, n_sub=n_sub, cs=cs, per=per,
                             levels_s=levels_s, n_slabs=n_slabs)
    rb = lambda b, n: b * steps + n
    sb = lambda b, n: n_p // srows + rb(b, n)
    return pl.pallas_call(
        kern,
        grid=(batch, steps),
        in_specs=[
            pl.BlockSpec((rows_per_step, Q_W), lambda b, n: (rb(b, n), Z_Q // Q_W)),
            pl.BlockSpec((rows_per_step, Q_W), lambda b, n: (rb(b, n), Z_K // Q_W)),
            pl.BlockSpec((rows_per_step, V_W), lambda b, n: (rb(b, n), Z_V // V_W)),
            pl.BlockSpec((rows_per_step, V_W), lambda b, n: (rb(b, n), Z_R // V_W)),
            pl.BlockSpec((rows_per_step, Q_W), lambda b, n: (rb(b, n), 0)),
            pl.BlockSpec((srows, Q_W), lambda b, n: (sb(b, n), Z_Q // Q_W)),
            pl.BlockSpec((srows, Q_W), lambda b, n: (sb(b, n), Z_K // Q_W)),
            pl.BlockSpec((srows, V_W), lambda b, n: (sb(b, n), Z_V // V_W)),
            pl.BlockSpec((srows, V_W), lambda b, n: (sb(b, n), Z_R // V_W)),
            pl.BlockSpec((srows, Q_W), lambda b, n: (sb(b, n), 0)),
            pl.BlockSpec((sseqs, GLA_HEADS, GLA_DK, GLA_DV), lambda b, n: (rb(b, n), 0, 0, 0)),
            pl.BlockSpec((1, GLA_DV), lambda b, n: (0, 0)),
        ] + _const_specs(consts_p) + _const_specs(consts_s),
        out_specs=[
            pl.BlockSpec((rows_per_step, V_W), lambda b, n: (rb(b, n), 0)),
            pl.BlockSpec((1, GLA_HEADS, GLA_DK, GLA_DV), lambda b, n: (b, 0, 0, 0)),
            pl.BlockSpec((srows, V_W), lambda b, n: (rb(b, n), 0)),
            pl.BlockSpec((sseqs, GLA_HEADS, GLA_DK, GLA_DV), lambda b, n: (rb(b, n), 0, 0, 0)),
        ],
        out_shape=[
            jax.ShapeDtypeStruct((n_p, V_W), _BF16),
            jax.ShapeDtypeStruct((batch, GLA_HEADS, GLA_DK, GLA_DV), _F32),
            jax.ShapeDtypeStruct((dec_batch * dec_seq, V_W), _BF16),
            jax.ShapeDtypeStruct((dec_batch, GLA_HEADS, GLA_DK, GLA_DV), _F32),
        ],
        scratch_shapes=[pltpu.VMEM((GLA_HEADS, GLA_DK, GLA_DV), _F32)],
        compiler_params=_params("arbitrary", "arbitrary"),
        name="gla",
    )(z, z, z, z, la, z, z, z, z, la, state, gla_g, *consts_p, *consts_s)


def _window_sums(ext_ref, t0, tt, lead):
    axis = len(lead)
    outs = []
    for gi, w in enumerate(POOL_WINDOWS):
        assert w & (w - 1) == 0 and w - 1 <= HALO
        lanes = slice(gi * POOL_GROUP_W, (gi + 1) * POOL_GROUP_W)
        s = ext_ref[lead + (slice(t0 - HALO, t0 + tt), lanes)]
        span = 1
        while span < w:
            s = s + pltpu.roll(s, span, axis=axis)
            span *= 2
        outs.append(s[lead + (slice(HALO, HALO + tt),)])
    return outs


def _pool_prompt_rows(ext_scr, pos0, r0, nr):
    sums = _window_sums(ext_scr, HALO + r0, nr, ())
    pos = (pos0 + r0 + 1 + lax.broadcasted_iota(jnp.int32, (nr, 1), 0)).astype(_F32)
    ps = []
    for gi, w in enumerate(POOL_WINDOWS):
        lanes = slice(gi * POOL_GROUP_W, (gi + 1) * POOL_GROUP_W)
        ps.append(sums[gi] / jnp.minimum(pos, float(w)) - ext_scr[HALO + r0:HALO + r0 + nr, lanes])
    return ps


def _pool_sample_tile(u_ref, buf_ref, nb_ref, ext_scr, bb, seq):
    u = u_ref[...].reshape(bb, seq, POOL_WIDTH)
    ext_scr[:, 0:HALO - POOL_BUF, :] = jnp.zeros((bb, HALO - POOL_BUF, POOL_WIDTH), _F32)
    for t in range(POOL_BUF):
        ext_scr[:, HALO - POOL_BUF + t, :] = buf_ref[t]
    ext_scr[:, HALO:HALO + seq, :] = u
    sums = _window_sums(ext_scr, HALO, seq, (slice(None),))
    ps = []
    for gi, w in enumerate(POOL_WINDOWS):
        lanes = slice(gi * POOL_GROUP_W, (gi + 1) * POOL_GROUP_W)
        ps.append((sums[gi] / float(w) - u[:, :, lanes]).reshape(bb * seq, POOL_GROUP_W))
    for t in range(POOL_BUF):
        nb_ref[t] = ext_scr[:, HALO + seq - POOL_BUF + t, :]
    return ps


def _merge_gate(rows, oa_ref, ps, ga_ref, gb_ref, pw_ref, ps_ref):
    ob = jnp.concatenate([_dot(ps[gi].astype(_BF16), pw_ref[gi]) for gi in range(POOL_GROUPS)],
                         axis=1) * ps_ref[...]
    m = (jax.nn.sigmoid(ga_ref[rows, :].astype(_F32)) * oa_ref[rows, :].astype(_F32)
         + jax.nn.sigmoid(gb_ref[rows, :].astype(_F32)) * ob)
    return m.astype(_BF16)


def _merge_project(rows, x_ref, m, wo_ref, g2_ref, x1_ref, h2_ref):
    x1 = x_ref[rows, :] + _dot(m, wo_ref[...])
    x1_ref[rows, :] = x1
    h2_ref[rows, :] = (x1 * _rms_scale(x1) * g2_ref[...]).astype(_BF16)


def _merge_core(rows, x_ref, oa_ref, ps, ga_ref, gb_ref, pw_ref, ps_ref, wo_ref, g2_ref,
                x1_ref, h2_ref):
    m = _merge_gate(rows, oa_ref, ps, ga_ref, gb_ref, pw_ref, ps_ref)
    _merge_project(rows, x_ref, m, wo_ref, g2_ref, x1_ref, h2_ref)


def _merge_prompt_kernel(x_ref, oa_ref, u_ref, ga_ref, gb_ref, pw_ref, ps_ref, wo_ref, g2_ref,
                         x1_ref, h2_ref, ext_scr, *, tt, sub):
    n = pl.program_id(1)

    @pl.when(n == 0)
    def _():
        ext_scr[0:HALO, :] = jnp.zeros((HALO, POOL_WIDTH), _F32)

    ext_scr[HALO:HALO + tt, :] = u_ref[...]
    for r0 in range(0, tt, sub):
        ps = _pool_prompt_rows(ext_scr, n * tt, r0, sub)
        _merge_core(slice(r0, r0 + sub), x_ref, oa_ref, ps, ga_ref, gb_ref, pw_ref, ps_ref, wo_ref,
                    g2_ref, x1_ref, h2_ref)
    ext_scr[0:HALO, :] = ext_scr[tt:tt + HALO, :]


def _merge_sample_kernel(x_ref, oa_ref, u_ref, buf_ref, ga_ref, gb_ref, pw_ref, ps_ref, wo_ref,
                         g2_ref, x1_ref, h2_ref, nb_ref, ext_scr, *, bb, seq):
    ps = _pool_sample_tile(u_ref, buf_ref, nb_ref, ext_scr, bb, seq)
    _merge_core(slice(None), x_ref, oa_ref, ps, ga_ref, gb_ref, pw_ref, ps_ref, wo_ref, g2_ref,
                x1_ref, h2_ref)


def _weight_specs():
    zeros = lambda nd: (lambda *idx: (0,) * nd)
    once = pl.Buffered(1)
    return [
        pl.BlockSpec((POOL_GROUPS, POOL_GROUP_W, POOL_OUT_GROUP_W), zeros(3), pipeline_mode=once),
        pl.BlockSpec((1, D_MODEL), zeros(2)),
        pl.BlockSpec((D_MODEL, D_MODEL), zeros(2), pipeline_mode=once),
        pl.BlockSpec((1, D_MODEL), zeros(2)),
    ]


def _merge_prompt(x, oa, u, zg, pool_w, pool_scale, w_out, g2, batch, seq, tt):
    steps = seq // tt
    rb = lambda b, n: b * steps + n
    return pl.pallas_call(
        functools.partial(_merge_prompt_kernel, tt=tt, sub=256),
        grid=(batch, steps),
        in_specs=[
            pl.BlockSpec((tt, D_MODEL), lambda b, n: (rb(b, n), 0)),
            pl.BlockSpec((tt, D_MODEL), lambda b, n: (rb(b, n), 0)),
            pl.BlockSpec((tt, POOL_WIDTH), lambda b, n: (rb(b, n), 0)),
            pl.BlockSpec((tt, D_MODEL), lambda b, n: (rb(b, n), Z_GA // D_MODEL)),
            pl.BlockSpec((tt, D_MODEL), lambda b, n: (rb(b, n), Z_GB // D_MODEL)),
        ] + _weight_specs(),
        out_specs=[
            pl.BlockSpec((tt, D_MODEL), lambda b, n: (rb(b, n), 0)),
            pl.BlockSpec((tt, D_MODEL), lambda b, n: (rb(b, n), 0)),
        ],
        out_shape=[
            jax.ShapeDtypeStruct((batch * seq, D_MODEL), _F32),
            jax.ShapeDtypeStruct((batch * seq, D_MODEL), _BF16),
        ],
        scratch_shapes=[pltpu.VMEM((HALO + tt, POOL_WIDTH), _F32)],
        compiler_params=_params("arbitrary", "arbitrary"),
        name="merge_prompt",
    )(x, oa, u, zg, zg, pool_w, pool_scale, w_out, g2)


def _merge_sample(x, oa, u, buf, zg, row0, pool_w, pool_scale, w_out, g2, seq, bb):
    batch = buf.shape[1]
    rows = bb * seq
    rb0 = row0 // rows
    assert PAST_LEN >= max(POOL_WINDOWS) and row0 % rows == 0 and seq == 8
    buf_spec = pl.BlockSpec((POOL_BUF, bb, POOL_WIDTH), lambda i: (0, i, 0))
    return pl.pallas_call(
        functools.partial(_merge_sample_kernel, bb=bb, seq=seq),
        grid=(batch // bb,),
        in_specs=[
            pl.BlockSpec((rows, D_MODEL), lambda i: (i, 0)),
            pl.BlockSpec((rows, D_MODEL), lambda i: (i, 0)),
            pl.BlockSpec((rows, POOL_WIDTH), lambda i: (rb0 + i, 0)),
            buf_spec,
            pl.BlockSpec((rows, D_MODEL), lambda i: (rb0 + i, Z_GA // D_MODEL)),
            pl.BlockSpec((rows, D_MODEL), lambda i: (rb0 + i, Z_GB // D_MODEL)),
        ] + _weight_specs(),
        out_specs=[
            pl.BlockSpec((rows, D_MODEL), lambda i: (i, 0)),
            pl.BlockSpec((rows, D_MODEL), lambda i: (i, 0)),
            buf_spec,
        ],
        out_shape=[
            jax.ShapeDtypeStruct((batch * seq, D_MODEL), _F32),
            jax.ShapeDtypeStruct((batch * seq, D_MODEL), _BF16),
            jax.ShapeDtypeStruct((POOL_BUF, batch, POOL_WIDTH), _F32),
        ],
        scratch_shapes=[pltpu.VMEM((bb, HALO + seq, POOL_WIDTH), _F32)],
        compiler_params=_params("arbitrary"),
        name="merge_sample",
    )(x, oa, u, buf, zg, zg, pool_w, pool_scale, w_out, g2)


def _mlp_kernel(h2_ref, x1_hbm, wu_ref, wd_ref, g_ref, y_ref, x1_scr, x1_sem):
    i = pl.program_id(0)
    f = pl.program_id(1)
    tm = y_ref.shape[0]

    def x1_copy():
        return pltpu.make_async_copy(x1_hbm.at[pl.ds(pl.multiple_of(i * tm, tm), tm), :],
                                     x1_scr, x1_sem)

    @pl.when(f == 0)
    def _():
        x1_copy().start()
        y_ref[...] = jnp.zeros_like(y_ref)

    a = jnp.maximum(_dot(h2_ref[...], wu_ref[...]), 0.0)
    y_ref[...] += _dot((a * a).astype(_BF16), wd_ref[...])

    @pl.when(f == pl.num_programs(1) - 1)
    def _():
        x1_copy().wait()
        x2 = y_ref[...] + x1_scr[...]
        y_ref[...] = x2 * _rms_scale(x2) * g_ref[...]


def _mlp(h2, x1, w_up, w_down, g, tm, tf):
    m = h2.shape[0]
    assert m % tm == 0
    return pl.pallas_call(
        _mlp_kernel,
        grid=(m // tm, D_FF // tf),
        in_specs=[
            pl.BlockSpec((tm, D_MODEL), lambda i, f: (i, 0)),
            pl.BlockSpec(memory_space=pl.ANY),
            pl.BlockSpec((D_MODEL, tf), lambda i, f: (0, f)),
            pl.BlockSpec((tf, D_MODEL), lambda i, f: (f, 0)),
            pl.BlockSpec((1, D_MODEL), lambda i, f: (0, 0)),
        ],
        out_specs=pl.BlockSpec((tm, D_MODEL), lambda i, f: (i, 0)),
        out_shape=jax.ShapeDtypeStruct((m, D_MODEL), _F32),
        scratch_shapes=[pltpu.VMEM((tm, D_MODEL), _F32), pltpu.SemaphoreType.DMA(())],
        compiler_params=_params("arbitrary", "arbitrary"),
        name="mlp",
    )(h2, x1, w_up, w_down, g)


def kernel(x_prompt, x_sample, state_gla, state_pool, norm_mix_g, w_in, w_alpha_up, b_alpha,
           gla_norm_g, pool_w, pool_scale, w_out, norm_mlp_g, w_up, w_down, norm_final_g):
    batch, seq, _ = x_prompt.shape
    dec_batch, dec_seq, _ = x_sample.shape
    n_p = batch * seq
    n_s = dec_batch * dec_seq
    xp = x_prompt.reshape(n_p, D_MODEL)
    xs = x_sample.reshape(n_s, D_MODEL)
    row = lambda v: v.reshape(1, -1).astype(_F32)

    wup_pad = jnp.pad(w_alpha_up, ((0, LANES - GATE_RANK), (0, 0))).astype(_BF16)
    wup3 = jnp.concatenate([wup_pad] * 3, axis=0)
    wt = w_in.T

    h, la, u, w_out_b, pool_w_b = _norm_gate_u(xp, xs, row(norm_mix_g), wt, wup3, row(b_alpha),
                                               w_out, pool_w, tm=512)
    z, w_up_b, w_down_b = _proj(
        h, wt, ((0, QKVR_W // PROJ_TN), (GA_OFF, 2 * D_MODEL // PROJ_TN)), _BF16, 1024,
        ((w_up, (512, 1024)), (w_down, (256, D_MODEL))))

    oa_p, s_gla_p, oa_s, s_gla_s = _gla(z, la, state_gla, row(gla_norm_g), batch, seq, dec_seq,
                                        c=64, rows_per_step=128)
    buf_p = jnp.stack([u[(b + 1) * seq - POOL_BUF:(b + 1) * seq] for b in range(batch)], axis=0)

    merge_w = (pool_w_b, row(pool_scale), w_out_b, row(norm_mlp_g))
    x1_p, h2_p = _merge_prompt(xp, oa_p, u, z, *merge_w, batch, seq, tt=512)
    x1_s, h2_s, buf_t = _merge_sample(xs, oa_s, u, state_pool.transpose(1, 0, 2), z, n_p, *merge_w,
                                      dec_seq, bb=32)
    buf_s = buf_t.transpose(1, 0, 2)
    mlp_w = (w_up_b, w_down_b, row(norm_final_g))
    y_p = _mlp(h2_p, x1_p, *mlp_w, tm=1024, tf=512).reshape(batch, seq, D_MODEL)
    y_s = _mlp(h2_s, x1_s, *mlp_w, tm=1024, tf=512).reshape(dec_batch, dec_seq, D_MODEL)
    return (y_p, y_s, s_gla_p, buf_p, s_gla_s, buf_s)
```

```python
import functools

import jax
import jax.numpy as jnp
import numpy as np
from jax import lax
from jax.experimental import pallas as pl
from jax.experimental.pallas import tpu as pltpu

D_MODEL = 2048
GLA_HEADS = 4
GLA_DK = 256
GLA_DV = 512
GATE_RANK = 16
GATE_TAU = 16.0
POOL_WIDTH = 1024
POOL_GROUPS = 4
POOL_GROUP_W = 256
POOL_OUT_GROUP_W = 512
POOL_WINDOWS = (2, 4, 8, 16)
POOL_BUF = 15
D_FF = 8192
EPS = 1e-6
PAST_LEN = 16384
LOG2_E = 1.4426950408889634

Q_W = GLA_HEADS * GLA_DK
V_W = GLA_HEADS * GLA_DV
QKVR_W = 2 * Q_W + 2 * V_W
A_OFF = QKVR_W
U_OFF = A_OFF + GATE_RANK
GA_OFF = U_OFF + POOL_WIDTH
Z_Q, Z_K, Z_V, Z_R = 0, Q_W, 2 * Q_W, 2 * Q_W + V_W
Z_GA, Z_GB = QKVR_W, QKVR_W + D_MODEL

LANES = 128
PROJ_TN = 1024
HALO = 16
VMEM_LIMIT = 56 * 1024 * 1024

_F32 = jnp.float32
_BF16 = jnp.bfloat16


def _dot(a, b):
    return jnp.dot(a, b, preferred_element_type=_F32)


def _dot_nt(a, b):
    return lax.dot_general(a, b, (((1,), (1,)), ((), ())), preferred_element_type=_F32)


def _dot_tn(a, b):
    return lax.dot_general(a, b, (((0,), (0,)), ((), ())), preferred_element_type=_F32)


def _split3(x):
    hi = x.astype(_BF16)
    r1 = x - hi.astype(_F32)
    mid = r1.astype(_BF16)
    lo = (r1 - mid.astype(_F32)).astype(_BF16)
    return hi, mid, lo


def _rms_scale(x):
    return lax.rsqrt(jnp.mean(x * x, axis=-1, keepdims=True) + EPS)


def _params(*sem):
    return pltpu.CompilerParams(dimension_semantics=sem, vmem_limit_bytes=VMEM_LIMIT)


def _norm_gate_u_kernel(xp_ref, xs_ref, g_ref, wa_ref, wup_ref, ba_ref, wu_ref, wo_ref, pw_ref,
                        h_ref, la_ref, u_ref, wo_out_ref, pw_out_ref, wu_scr,
                        *, prompt_tiles, sub, n_side):
    i = pl.program_id(0)
    rc = 256

    @pl.when(i == 0)
    def _():
        for r in range(0, POOL_WIDTH, rc):
            wu_scr[r:r + rc, :] = wu_ref[r:r + rc, :].astype(_BF16)
        pw_out_ref[...] = pw_ref[...].astype(_BF16)

    @pl.when(i < n_side)
    def _():
        wo_out_ref[...] = wo_ref[...].astype(_BF16)

    def compute(x_ref):
        wa = wa_ref[...].astype(_BF16)
        for r0 in range(0, x_ref.shape[0], sub):
            rows = slice(r0, r0 + sub)
            x = x_ref[rows, :]
            hb = (x * _rms_scale(x) * g_ref[...]).astype(_BF16)
            h_ref[rows, :] = hb
            u_ref[rows, :] = _dot_nt(hb, wu_scr[...])
            a_lr = _dot_nt(hb, wa)
            a_hi, a_mid, a_lo = _split3(a_lr)
            a3 = jnp.concatenate([a_hi, a_mid, a_lo], axis=1)
            a_logit = _dot(a3, wup_ref[...]) + ba_ref[...]
            ls = jnp.minimum(a_logit, 0.0) - jnp.log1p(jnp.exp(-jnp.abs(a_logit)))
            la_ref[rows, :] = ls * (LOG2_E / GATE_TAU)

    @pl.when(i < prompt_tiles)
    def _():
        compute(xp_ref)

    @pl.when(i >= prompt_tiles)
    def _():
        compute(xs_ref)


def _norm_gate_u(xp, xs, g, wt, wup3, b_alpha, w_out, pool_w, tm):
    n_p, n_s = xp.shape[0], xs.shape[0]
    pt, st = n_p // tm, n_s // tm
    m = n_p + n_s
    const = lambda i: (0, 0)
    wo_rows = 128
    n_side = w_out.shape[0] // wo_rows
    assert n_side <= pt + st and n_side * wo_rows == w_out.shape[0]
    wo_spec = pl.BlockSpec((wo_rows, D_MODEL), lambda i: (jnp.minimum(i, n_side - 1), 0))
    pw_spec = pl.BlockSpec(pool_w.shape, lambda i: (0, 0, 0))
    return pl.pallas_call(
        functools.partial(_norm_gate_u_kernel, prompt_tiles=pt, sub=min(tm, 256), n_side=n_side),
        grid=(pt + st,),
        in_specs=[
            pl.BlockSpec((tm, D_MODEL), lambda i: (jnp.minimum(i, pt - 1), 0)),
            pl.BlockSpec((tm, D_MODEL), lambda i: (jnp.maximum(i - pt, 0), 0)),
            pl.BlockSpec((1, D_MODEL), const),
            pl.BlockSpec((LANES, D_MODEL), lambda i: (A_OFF // LANES, 0)),
            pl.BlockSpec((3 * LANES, Q_W), const),
            pl.BlockSpec((1, Q_W), const),
            pl.BlockSpec((pl.Element(POOL_WIDTH), pl.Element(D_MODEL)), lambda i: (U_OFF, 0),
                         pipeline_mode=pl.Buffered(1)),
            wo_spec,
            pw_spec,
        ],
        out_specs=[
            pl.BlockSpec((tm, D_MODEL), lambda i: (i, 0)),
            pl.BlockSpec((tm, Q_W), lambda i: (i, 0)),
            pl.BlockSpec((tm, POOL_WIDTH), lambda i: (i, 0)),
            wo_spec,
            pw_spec,
        ],
        out_shape=[
            jax.ShapeDtypeStruct((m, D_MODEL), _BF16),
            jax.ShapeDtypeStruct((m, Q_W), _F32),
            jax.ShapeDtypeStruct((m, POOL_WIDTH), _F32),
            jax.ShapeDtypeStruct(w_out.shape, _BF16),
            jax.ShapeDtypeStruct(pool_w.shape, _BF16),
        ],
        scratch_shapes=[pltpu.VMEM((POOL_WIDTH, D_MODEL), _BF16)],
        compiler_params=_params("arbitrary"),
        name="norm_gate_u",
    )(xp, xs, g, wt, wup3, b_alpha, wt, w_out, pool_w)


def _proj_kernel(h_ref, wt_ref, *refs, side_steps):
    n_s = len(side_steps)
    side_refs, z_ref, side_out_refs, w_scr = refs[:n_s], refs[n_s], refs[n_s + 1:-1], refs[-1]
    rc = 256

    @pl.when(pl.program_id(1) == 0)
    def _():
        for r in range(0, w_scr.shape[0], rc):
            w_scr[r:r + rc, :] = wt_ref[r:r + rc, :].astype(_BF16)

    z_ref[...] = _dot_nt(h_ref[...], w_scr[...]).astype(z_ref.dtype)

    step = pl.program_id(0) * pl.num_programs(1) + pl.program_id(1)
    for (lo, hi), src, dst in zip(side_steps, side_refs, side_out_refs):
        @pl.when((step >= lo) & (step < hi))
        def _(src=src, dst=dst):
            dst[...] = src[...].astype(_BF16)


def _proj(h, wt, pieces, out_dtype, tm, sides, tn=PROJ_TN):
    m = h.shape[0]
    (col0, n0), (col1, n1) = pieces
    n_tiles = n0 + n1
    n_i = m // tm
    assert all(c % 8 == 0 and c + n * tn <= wt.shape[0] for c, n in pieces)

    side_specs, side_steps, first = [], [], 0
    for arr, blk in sides:
        sr, sc = arr.shape[0] // blk[0], arr.shape[1] // blk[1]
        assert sr * blk[0] == arr.shape[0] and sc * blk[1] == arr.shape[1]
        n_blk = sr * sc

        def side_idx(j, i, first=first, n_blk=n_blk, sc=sc):
            b = jnp.clip(j * n_i + i - first, 0, n_blk - 1)
            return (b // sc, b % sc)

        side_specs.append(pl.BlockSpec(blk, side_idx))
        side_steps.append((first, first + n_blk))
        first += n_blk
    assert first <= n_tiles * n_i

    def w_idx(j, i):
        start = jnp.where(j < n0, col0 + j * tn, col1 + (j - n0) * tn)
        return (pl.multiple_of(start, 8), 0)

    return pl.pallas_call(
        functools.partial(_proj_kernel, side_steps=tuple(side_steps)),
        grid=(n_tiles, n_i),
        in_specs=[
            pl.BlockSpec((tm, D_MODEL), lambda j, i: (i, 0)),
            pl.BlockSpec((pl.Element(tn), pl.Element(D_MODEL)), w_idx),
        ] + side_specs,
        out_specs=[pl.BlockSpec((tm, tn), lambda j, i: (i, j))] + side_specs,
        out_shape=[jax.ShapeDtypeStruct((m, n_tiles * tn), out_dtype)]
        + [jax.ShapeDtypeStruct(arr.shape, _BF16) for arr, _ in sides],
        scratch_shapes=[pltpu.VMEM((tn, D_MODEL), _BF16)],
        compiler_params=_params("arbitrary", "arbitrary"),
        name="proj",
    )(h, wt, *[arr for arr, _ in sides])


def _gla_consts(c, per=1):
    g = c // per
    levels = int(np.log2(g))
    assert 2 ** levels == g and g * per == c
    t = np.arange(c)
    seq_id = t // g
    mats = [np.tril(np.ones((c, c), np.float32)) * (seq_id[:, None] == seq_id[None, :])]
    masks = [np.eye(c, dtype=np.float32)]
    for l in range(levels):
        h = 2 ** l
        blk = t // (2 * h)
        mid = blk * 2 * h + h - 1
        upper = (t // h) % 2 == 1
        m = np.zeros((c, c), np.float32)
        for row in range(c):
            if upper[row]:
                m[row, mid[row] + 1: row + 1] = 1.0
            else:
                m[row, row + 1: mid[row] + 1] = 1.0
        mats.append(m)
        masks.append(((blk[:, None] == blk[None, :]) & upper[:, None] & (~upper)[None, :])
                     .astype(np.float32))
    m_all = np.concatenate(mats, axis=0)
    m_cat = np.concatenate([m_all, m_all, m_all], axis=1)
    rowmasks = np.stack([(seq_id == e).astype(np.float32)[:, None] for e in range(per)])
    ones_cols = np.stack([np.tile(np.broadcast_to(rowmasks[e], (c, LANES)), (3, 1))
                          for e in range(per)])
    return (jnp.asarray(m_cat, _BF16), jnp.asarray(np.stack(masks), _F32),
            jnp.asarray(ones_cols, _BF16), jnp.asarray(rowmasks, _F32), levels)


def _gla_post(o, r, g):
    o_n = o * _rms_scale(o) * g
    return o_n * (r * jax.nn.sigmoid(r))


def _gla_rows(q_ref, k_ref, v_ref, r_ref, la_ref, oa_ref, rows, load_state, store_state, consts,
              c, levels, g):
    m_cat, masks, ones_cols, rowmasks = consts
    heads = range(GLA_HEADS)
    per = len(ones_cols)
    gsz = c // per
    dk = lambda h: slice(h * GLA_DK, (h + 1) * GLA_DK)
    dv = lambda h: slice(h * GLA_DV, (h + 1) * GLA_DV)
    sl = lambda a, h: a[:, dk(h)]
    qs = [q_ref[rows, dk(h)].astype(_F32) * (GLA_DK ** -0.5) for h in heads]
    ks = [k_ref[rows, dk(h)].astype(_F32) for h in heads]
    vbs = [v_ref[rows, dv(h)] for h in heads]
    hi, mid, lo = _split3(la_ref[rows, :])
    la3 = jnp.concatenate([hi, mid, lo], axis=0)
    eb = _dot(m_cat, la3)
    dcols = [jnp.exp2(_dot_tn(la3, ones_cols[e])) for e in range(per)]
    yield
    b = eb[0:c]
    scores = [jnp.where(masks[0], _dot_nt(qs[h].astype(_BF16), ks[h].astype(_BF16)), 0.0)
              for h in heads]
    for l in range(levels):
        x = jnp.exp2(eb[(l + 1) * c:(l + 2) * c])
        for h in heads:
            xh = sl(x, h)
            s_l = _dot_nt((qs[h] * xh).astype(_BF16), (ks[h] * xh).astype(_BF16))
            scores[h] = jnp.where(masks[l + 1], s_l, scores[h])
        yield
    eb0 = jnp.exp2(b)
    ebl = jnp.exp2(jnp.concatenate(
        [b[(e + 1) * gsz - 1:(e + 1) * gsz, :] - b[e * gsz:(e + 1) * gsz, :] for e in range(per)],
        axis=0))
    pick = (lambda a, e: a) if per == 1 else (lambda a, e: a * rowmasks[e])
    states = [[load_state(h, e) for e in range(per)] for h in heads]
    outs = []
    for h in heads:
        o = _dot(scores[h].astype(_BF16), vbs[h])
        qd = qs[h] * sl(eb0, h)
        for e in range(per):
            o = o + _dot(pick(qd, e).astype(_BF16), states[h][e].astype(_BF16))
        outs.append(o)
    yield
    for h in heads:
        kd = ks[h] * sl(ebl, h)
        for e in range(per):
            dh = dcols[e][dk(h)]
            decay = jnp.concatenate([dh] * (GLA_DV // LANES), axis=1)
            store_state(h, e, decay * states[h][e] + _dot_tn(pick(kd, e).astype(_BF16), vbs[h]))
    yield
    for h in heads:
        oa_ref[rows, dv(h)] = _gla_post(outs[h], r_ref[rows, dv(h)].astype(_F32), g
                                        ).astype(oa_ref.dtype)


def _run_staggered(tasks):
    live = list(tasks)
    t = 0
    while live:
        for item in list(live):
            start, gen = item
            if start <= t:
                try:
                    next(gen)
                except StopIteration:
                    live.remove(item)
        t += 1


def _load_gla_consts(mc_ref, mk_ref, oc_ref, rm_ref):
    return (mc_ref[...], [mk_ref[i] > 0 for i in range(mk_ref.shape[0])],
            [oc_ref[e] for e in range(oc_ref.shape[0])], [rm_ref[e] for e in range(rm_ref.shape[0])])


def _const_specs(arrays):
    return [pl.BlockSpec(a.shape, lambda *idx, nd=a.ndim: (0,) * nd) for a in arrays]


def _gla_kernel(qp_ref, kp_ref, vp_ref, rp_ref, lap_ref, qs_ref, ks_ref, vs_ref, rs_ref, las_ref,
                st_ref, g_ref, mcp_ref, mkp_ref, ocp_ref, rmp_ref, mcs_ref, mks_ref, ocs_ref, rms_ref,
                wu_ref, wd_ref, oap_ref, soutp_ref, oas_ref, souts_ref, wu_out_ref, wd_out_ref, s_scr,
                *, c, levels, n_sub, cs, per, levels_s, n_slabs, side_steps):
    n = pl.program_id(1)

    step = pl.program_id(0) * pl.num_programs(1) + n
    for (lo, hi), src, dst in zip(side_steps, (wu_ref, wd_ref), (wu_out_ref, wd_out_ref)):
        @pl.when((step >= lo) & (step < hi))
        def _(src=src, dst=dst):
            dst[...] = src[...].astype(_BF16)

    @pl.when(n == 0)
    def _():
        s_scr[...] = jnp.zeros_like(s_scr)

    consts_p = _load_gla_consts(mcp_ref, mkp_ref, ocp_ref, rmp_ref)
    consts_s = _load_gla_consts(mcs_ref, mks_ref, ocs_ref, rms_ref)
    g = g_ref[...]

    def prompt_load(h, e):
        return s_scr[h]

    def prompt_store(h, e, value):
        s_scr[h] = value

    every = n_sub // n_slabs
    for i0 in range(0, n_sub, every):
        j = i0 // every

        def slab_load(h, e, j=j):
            return st_ref[j * per + e, h]

        def slab_store(h, e, value, j=j):
            souts_ref[j * per + e, h] = value

        tasks = [(2 * k, _gla_rows(qp_ref, kp_ref, vp_ref, rp_ref, lap_ref, oap_ref,
                                   slice((i0 + k) * c, (i0 + k + 1) * c), prompt_load, prompt_store,
                                   consts_p, c, levels, g)) for k in range(every)]
        tasks.insert(1, (1, _gla_rows(qs_ref, ks_ref, vs_ref, rs_ref, las_ref, oas_ref,
                                      slice(j * cs, (j + 1) * cs), slab_load, slab_store, consts_s,
                                      cs, levels_s, g)))
        _run_staggered(tasks)

    @pl.when(n == pl.num_programs(1) - 1)
    def _():
        soutp_ref[0] = s_scr[...]


def _gla(z, la, state, gla_g, batch, seq, dec_seq, c, rows_per_step, sides):
    cs = 16
    per = cs // dec_seq
    dec_batch = state.shape[0]
    n_p = batch * seq
    steps = seq // rows_per_step
    n_sub = rows_per_step // c
    n_slabs = dec_batch // per // (batch * steps)
    assert cs % dec_seq == 0 and batch * steps * n_slabs * per == dec_batch and n_p % cs == 0
    assert n_sub % n_slabs == 0
    srows, sseqs = n_slabs * cs, n_slabs * per
    *consts_p, levels = _gla_consts(c)
    *consts_s, levels_s = _gla_consts(cs, per)
    rb = lambda b, n: b * steps + n
    sb = lambda b, n: n_p // srows + rb(b, n)

    side_specs, side_steps, first = [], [], 0
    for arr, blk in sides:
        sr, sc = arr.shape[0] // blk[0], arr.shape[1] // blk[1]
        assert sr * blk[0] == arr.shape[0] and sc * blk[1] == arr.shape[1]
        n_blk = sr * sc

        def side_idx(b, n, first=first, n_blk=n_blk, sc=sc):
            k = jnp.clip(rb(b, n) - first, 0, n_blk - 1)
            return (k // sc, k % sc)

        side_specs.append(pl.BlockSpec(blk, side_idx))
        side_steps.append((first, first + n_blk))
        first += n_blk
    assert len(sides) == 2 and first <= batch * steps
    kern = functools.partial(_gla_kernel, c=c, levels=levels, n_sub=n_sub, cs=cs, per=per,
                             levels_s=levels_s, n_slabs=n_slabs, side_steps=tuple(side_steps))
    return pl.pallas_call(
        kern,
        grid=(batch, steps),
        in_specs=[
            pl.BlockSpec((rows_per_step, Q_W), lambda b, n: (rb(b, n), Z_Q // Q_W)),
            pl.BlockSpec((rows_per_step, Q_W), lambda b, n: (rb(b, n), Z_K // Q_W)),
            pl.BlockSpec((rows_per_step, V_W), lambda b, n: (rb(b, n), Z_V // V_W)),
            pl.BlockSpec((rows_per_step, V_W), lambda b, n: (rb(b, n), Z_R // V_W)),
            pl.BlockSpec((rows_per_step, Q_W), lambda b, n: (rb(b, n), 0)),
            pl.BlockSpec((srows, Q_W), lambda b, n: (sb(b, n), Z_Q // Q_W)),
            pl.BlockSpec((srows, Q_W), lambda b, n: (sb(b, n), Z_K // Q_W)),
            pl.BlockSpec((srows, V_W), lambda b, n: (sb(b, n), Z_V // V_W)),
            pl.BlockSpec((srows, V_W), lambda b, n: (sb(b, n), Z_R // V_W)),
            pl.BlockSpec((srows, Q_W), lambda b, n: (sb(b, n), 0)),
            pl.BlockSpec((sseqs, GLA_HEADS, GLA_DK, GLA_DV), lambda b, n: (rb(b, n), 0, 0, 0)),
            pl.BlockSpec((1, GLA_DV), lambda b, n: (0, 0)),
        ] + _const_specs(consts_p) + _const_specs(consts_s) + side_specs,
        out_specs=[
            pl.BlockSpec((rows_per_step, V_W), lambda b, n: (rb(b, n), 0)),
            pl.BlockSpec((1, GLA_HEADS, GLA_DK, GLA_DV), lambda b, n: (b, 0, 0, 0)),
            pl.BlockSpec((srows, V_W), lambda b, n: (rb(b, n), 0)),
            pl.BlockSpec((sseqs, GLA_HEADS, GLA_DK, GLA_DV), lambda b, n: (rb(b, n), 0, 0, 0)),
        ] + side_specs,
        out_shape=[
            jax.ShapeDtypeStruct((n_p, V_W), _BF16),
            jax.ShapeDtypeStruct((batch, GLA_HEADS, GLA_DK, GLA_DV), _F32),
            jax.ShapeDtypeStruct((dec_batch * dec_seq, V_W), _BF16),
            jax.ShapeDtypeStruct((dec_batch, GLA_HEADS, GLA_DK, GLA_DV), _F32),
        ] + [jax.ShapeDtypeStruct(arr.shape, _BF16) for arr, _ in sides],
        scratch_shapes=[pltpu.VMEM((GLA_HEADS, GLA_DK, GLA_DV), _F32)],
        compiler_params=_params("arbitrary", "arbitrary"),
        name="gla",
    )(z, z, z, z, la, z, z, z, z, la, state, gla_g, *consts_p, *consts_s, *[arr for arr, _ in sides])


def _window_sums(ext_ref, t0, tt, lead):
    axis = len(lead)
    outs = []
    for gi, w in enumerate(POOL_WINDOWS):
        assert w & (w - 1) == 0 and w - 1 <= HALO
        lanes = slice(gi * POOL_GROUP_W, (gi + 1) * POOL_GROUP_W)
        s = ext_ref[lead + (slice(t0 - HALO, t0 + tt), lanes)]
        span = 1
        while span < w:
            s = s + pltpu.roll(s, span, axis=axis)
            span *= 2
        outs.append(s[lead + (slice(HALO, HALO + tt),)])
    return outs


def _pool_prompt_rows(ext_scr, pos0, r0, nr):
    sums = _window_sums(ext_scr, HALO + r0, nr, ())
    pos = (pos0 + r0 + 1 + lax.broadcasted_iota(jnp.int32, (nr, 1), 0)).astype(_F32)
    ps = []
    for gi, w in enumerate(POOL_WINDOWS):
        lanes = slice(gi * POOL_GROUP_W, (gi + 1) * POOL_GROUP_W)
        ps.append(sums[gi] / jnp.minimum(pos, float(w)) - ext_scr[HALO + r0:HALO + r0 + nr, lanes])
    return ps


def _pool_sample_tile(u_ref, buf_ref, nb_ref, ext_scr, bb, seq):
    u = u_ref[...].reshape(bb, seq, POOL_WIDTH)
    ext_scr[:, 0:HALO - POOL_BUF, :] = jnp.zeros((bb, HALO - POOL_BUF, POOL_WIDTH), _F32)
    for t in range(POOL_BUF):
        ext_scr[:, HALO - POOL_BUF + t, :] = buf_ref[t]
    ext_scr[:, HALO:HALO + seq, :] = u
    sums = _window_sums(ext_scr, HALO, seq, (slice(None),))
    ps = []
    for gi, w in enumerate(POOL_WINDOWS):
        lanes = slice(gi * POOL_GROUP_W, (gi + 1) * POOL_GROUP_W)
        ps.append((sums[gi] / float(w) - u[:, :, lanes]).reshape(bb * seq, POOL_GROUP_W))
    for t in range(POOL_BUF):
        nb_ref[t] = ext_scr[:, HALO + seq - POOL_BUF + t, :]
    return ps


def _merge_gate(rows, oa_ref, ps, ga_ref, gb_ref, pw_ref, ps_ref):
    ob = jnp.concatenate([_dot(ps[gi].astype(_BF16), pw_ref[gi]) for gi in range(POOL_GROUPS)],
                         axis=1) * ps_ref[...]
    m = (jax.nn.sigmoid(ga_ref[rows, :].astype(_F32)) * oa_ref[rows, :].astype(_F32)
         + jax.nn.sigmoid(gb_ref[rows, :].astype(_F32)) * ob)
    return m.astype(_BF16)


def _merge_project(rows, x_ref, m, wo_ref, g2_ref, x1_ref, h2_ref):
    x1 = x_ref[rows, :] + _dot(m, wo_ref[...])
    x1_ref[rows, :] = x1
    h2_ref[rows, :] = (x1 * _rms_scale(x1) * g2_ref[...]).astype(_BF16)


def _merge_core(rows, x_ref, oa_ref, ps, ga_ref, gb_ref, pw_ref, ps_ref, wo_ref, g2_ref,
                x1_ref, h2_ref):
    m = _merge_gate(rows, oa_ref, ps, ga_ref, gb_ref, pw_ref, ps_ref)
    _merge_project(rows, x_ref, m, wo_ref, g2_ref, x1_ref, h2_ref)


def _merge_prompt_kernel(x_ref, oa_ref, u_ref, ga_ref, gb_ref, pw_ref, ps_ref, wo_ref, g2_ref,
                         x1_ref, h2_ref, ext_scr, *, tt, sub):
    n = pl.program_id(1)

    @pl.when(n == 0)
    def _():
        ext_scr[0:HALO, :] = jnp.zeros((HALO, POOL_WIDTH), _F32)

    ext_scr[HALO:HALO + tt, :] = u_ref[...]
    for r0 in range(0, tt, sub):
        ps = _pool_prompt_rows(ext_scr, n * tt, r0, sub)
        _merge_core(slice(r0, r0 + sub), x_ref, oa_ref, ps, ga_ref, gb_ref, pw_ref, ps_ref, wo_ref,
                    g2_ref, x1_ref, h2_ref)
    ext_scr[0:HALO, :] = ext_scr[tt:tt + HALO, :]


def _merge_sample_kernel(x_ref, oa_ref, u_ref, buf_ref, ga_ref, gb_ref, pw_ref, ps_ref, wo_ref,
                         g2_ref, x1_ref, h2_ref, nb_ref, ext_scr, *, bb, seq):
    ps = _pool_sample_tile(u_ref, buf_ref, nb_ref, ext_scr, bb, seq)
    _merge_core(slice(None), x_ref, oa_ref, ps, ga_ref, gb_ref, pw_ref, ps_ref, wo_ref, g2_ref,
                x1_ref, h2_ref)


def _weight_specs():
    zeros = lambda nd: (lambda *idx: (0,) * nd)
    once = pl.Buffered(1)
    return [
        pl.BlockSpec((POOL_GROUPS, POOL_GROUP_W, POOL_OUT_GROUP_W), zeros(3), pipeline_mode=once),
        pl.BlockSpec((1, D_MODEL), zeros(2)),
        pl.BlockSpec((D_MODEL, D_MODEL), zeros(2), pipeline_mode=once),
        pl.BlockSpec((1, D_MODEL), zeros(2)),
    ]


def _merge_prompt(x, oa, u, zg, pool_w, pool_scale, w_out, g2, batch, seq, tt):
    steps = seq // tt
    rb = lambda b, n: b * steps + n
    return pl.pallas_call(
        functools.partial(_merge_prompt_kernel, tt=tt, sub=256),
        grid=(batch, steps),
        in_specs=[
            pl.BlockSpec((tt, D_MODEL), lambda b, n: (rb(b, n), 0)),
            pl.BlockSpec((tt, D_MODEL), lambda b, n: (rb(b, n), 0)),
            pl.BlockSpec((tt, POOL_WIDTH), lambda b, n: (rb(b, n), 0)),
            pl.BlockSpec((tt, D_MODEL), lambda b, n: (rb(b, n), Z_GA // D_MODEL)),
            pl.BlockSpec((tt, D_MODEL), lambda b, n: (rb(b, n), Z_GB // D_MODEL)),
        ] + _weight_specs(),
        out_specs=[
            pl.BlockSpec((tt, D_MODEL), lambda b, n: (rb(b, n), 0)),
            pl.BlockSpec((tt, D_MODEL), lambda b, n: (rb(b, n), 0)),
        ],
        out_shape=[
            jax.ShapeDtypeStruct((batch * seq, D_MODEL), _F32),
            jax.ShapeDtypeStruct((batch * seq, D_MODEL), _BF16),
        ],
        scratch_shapes=[pltpu.VMEM((HALO + tt, POOL_WIDTH), _F32)],
        compiler_params=_params("arbitrary", "arbitrary"),
        name="merge_prompt",
    )(x, oa, u, zg, zg, pool_w, pool_scale, w_out, g2)


def _merge_sample(x, oa, u, buf, zg, row0, pool_w, pool_scale, w_out, g2, seq, bb):
    batch = buf.shape[1]
    rows = bb * seq
    rb0 = row0 // rows
    assert PAST_LEN >= max(POOL_WINDOWS) and row0 % rows == 0 and seq == 8
    buf_spec = pl.BlockSpec((POOL_BUF, bb, POOL_WIDTH), lambda i: (0, i, 0))
    return pl.pallas_call(
        functools.partial(_merge_sample_kernel, bb=bb, seq=seq),
        grid=(batch // bb,),
        in_specs=[
            pl.BlockSpec((rows, D_MODEL), lambda i: (i, 0)),
            pl.BlockSpec((rows, D_MODEL), lambda i: (i, 0)),
            pl.BlockSpec((rows, POOL_WIDTH), lambda i: (rb0 + i, 0)),
            buf_spec,
            pl.BlockSpec((rows, D_MODEL), lambda i: (rb0 + i, Z_GA // D_MODEL)),
            pl.BlockSpec((rows, D_MODEL), lambda i: (rb0 + i, Z_GB // D_MODEL)),
        ] + _weight_specs(),
        out_specs=[
            pl.BlockSpec((rows, D_MODEL), lambda i: (i, 0)),
            pl.BlockSpec((rows, D_MODEL), lambda i: (i, 0)),
            buf_spec,
        ],
        out_shape=[
            jax.ShapeDtypeStruct((batch * seq, D_MODEL), _F32),
            jax.ShapeDtypeStruct((batch * seq, D_MODEL), _BF16),
            jax.ShapeDtypeStruct((POOL_BUF, batch, POOL_WIDTH), _F32),
        ],
        scratch_shapes=[pltpu.VMEM((bb, HALO + seq, POOL_WIDTH), _F32)],
        compiler_params=_params("arbitrary"),
        name="merge_sample",
    )(x, oa, u, buf, zg, zg, pool_w, pool_scale, w_out, g2)


def _mlp_kernel(h2_ref, x1_hbm, wu_ref, wd_ref, g_ref, y_ref, x1_scr, x1_sem):
    i = pl.program_id(0)
    f = pl.program_id(1)
    tm = y_ref.shape[0]

    def x1_copy():
        return pltpu.make_async_copy(x1_hbm.at[pl.ds(pl.multiple_of(i * tm, tm), tm), :],
                                     x1_scr, x1_sem)

    @pl.when(f == 0)
    def _():
        x1_copy().start()
        y_ref[...] = jnp.zeros_like(y_ref)

    a = jnp.maximum(_dot(h2_ref[...], wu_ref[...]), 0.0)
    y_ref[...] += _dot((a * a).astype(_BF16), wd_ref[...])

    @pl.when(f == pl.num_programs(1) - 1)
    def _():
        x1_copy().wait()
        x2 = y_ref[...] + x1_scr[...]
        y_ref[...] = x2 * _rms_scale(x2) * g_ref[...]


def _mlp(h2, x1, w_up, w_down, g, tm, tf):
    m = h2.shape[0]
    assert m % tm == 0
    return pl.pallas_call(
        _mlp_kernel,
        grid=(m // tm, D_FF // tf),
        in_specs=[
            pl.BlockSpec((tm, D_MODEL), lambda i, f: (i, 0)),
            pl.BlockSpec(memory_space=pl.ANY),
            pl.BlockSpec((D_MODEL, tf), lambda i, f: (0, f)),
            pl.BlockSpec((tf, D_MODEL), lambda i, f: (f, 0)),
            pl.BlockSpec((1, D_MODEL), lambda i, f: (0, 0)),
        ],
        out_specs=pl.BlockSpec((tm, D_MODEL), lambda i, f: (i, 0)),
        out_shape=jax.ShapeDtypeStruct((m, D_MODEL), _F32),
        scratch_shapes=[pltpu.VMEM((tm, D_MODEL), _F32), pltpu.SemaphoreType.DMA(())],
        compiler_params=_params("arbitrary", "arbitrary"),
        name="mlp",
    )(h2, x1, w_up, w_down, g)


def kernel(x_prompt, x_sample, state_gla, state_pool, norm_mix_g, w_in, w_alpha_up, b_alpha,
           gla_norm_g, pool_w, pool_scale, w_out, norm_mlp_g, w_up, w_down, norm_final_g):
    batch, seq, _ = x_prompt.shape
    dec_batch, dec_seq, _ = x_sample.shape
    n_p = batch * seq
    n_s = dec_batch * dec_seq
    xp = x_prompt.reshape(n_p, D_MODEL)
    xs = x_sample.reshape(n_s, D_MODEL)
    row = lambda v: v.reshape(1, -1).astype(_F32)

    wup_pad = jnp.pad(w_alpha_up, ((0, LANES - GATE_RANK), (0, 0))).astype(_BF16)
    wup3 = jnp.concatenate([wup_pad] * 3, axis=0)
    wt = w_in.T

    h, la, u, w_out_b, pool_w_b = _norm_gate_u(xp, xs, row(norm_mix_g), wt, wup3, row(b_alpha),
                                               w_out, pool_w, tm=512)
    z, = _proj(h, wt, ((0, QKVR_W // PROJ_TN), (GA_OFF, 2 * D_MODEL // PROJ_TN)), _BF16, 1024, ())

    oa_p, s_gla_p, oa_s, s_gla_s, w_up_b, w_down_b = _gla(
        z, la, state_gla, row(gla_norm_g), batch, seq, dec_seq, c=64, rows_per_step=128,
        sides=((w_up, (512, 1024)), (w_down, (256, D_MODEL))))
    buf_p = jnp.stack([u[(b + 1) * seq - POOL_BUF:(b + 1) * seq] for b in range(batch)], axis=0)

    merge_w = (pool_w_b, row(pool_scale), w_out_b, row(norm_mlp_g))
    x1_p, h2_p = _merge_prompt(xp, oa_p, u, z, *merge_w, batch, seq, tt=512)
    x1_s, h2_s, buf_t = _merge_sample(xs, oa_s, u, state_pool.transpose(1, 0, 2), z, n_p, *merge_w,
                                      dec_seq, bb=32)
    buf_s = buf_t.transpose(1, 0, 2)
    mlp_w = (w_up_b, w_down_b, row(norm_final_g))
    y_p = _mlp(h2_p, x1_p, *mlp_w, tm=1024, tf=512).reshape(batch, seq, D_MODEL)
    y_s = _mlp(h2_s, x1_s, *mlp_w, tm=1024, tf=512).reshape(dec_batch, dec_seq, D_MODEL)
    return (y_p, y_s, s_gla_p, buf_p, s_gla_s, buf_s)
```
